```python
import math
import jax, jax.numpy as jnp
from jax import lax
import numpy as np

D_MODEL = 1024
BATCH = 8
SEQ = 2048
DEPTH = 1
DEC_BATCH = 128
DEC_SEQ = 4
PAST_LEN = 16384
PAGE_SIZE = 128

SSD_EXPAND = 2
SSD_INNER = SSD_EXPAND * D_MODEL
SSD_HEAD_DIM = 64
SSD_HEADS = SSD_INNER // SSD_HEAD_DIM
SSD_GROUPS = 4
SSD_STATE = 128
SSD_CONV = 4
SSD_CHUNK = 128
SSD_CONV_DIM = SSD_INNER + 2 * SSD_GROUPS * SSD_STATE
HG_EXPAND = 128
HG_HEADS = D_MODEL // HG_EXPAND
HG_KDIM = HG_HEADS * HG_EXPAND
HG_VDIM = D_MODEL
HG_HEAD_V = HG_VDIM // HG_HEADS
HG_CHUNK = 32
MEM_LEN = 256
XA_HEADS = 4
XA_HEAD_DIM = D_MODEL // XA_HEADS
FFN_DIM = ((-(-8 * D_MODEL // 3)) + 255) // 256 * 256
EPS = 1e-6
F32 = jnp.float32
IN_SIZES = (SSD_INNER, SSD_CONV_DIM, SSD_HEADS, HG_KDIM, HG_KDIM, HG_VDIM, HG_VDIM, D_MODEL, D_MODEL)
IN_DIM = sum(IN_SIZES)
IN_SPLITS = tuple(int(v) for v in np.cumsum(IN_SIZES)[:-1])

kernel_name = 'hybrid_ssd_hgrn2_gated_decoder_step'


def group_rmsnorm(x, w, groups):
    shp = x.shape
    xg = x.astype(F32).reshape(shp[:-1] + (groups, shp[-1] // groups))
    xg = xg * lax.rsqrt(jnp.mean(xg * xg, axis=-1, keepdims=True) + EPS)
    return (xg.reshape(shp) * w.astype(F32)).astype(x.dtype)


def rmsnorm(x, w):
    return group_rmsnorm(x, w, 1)


def causal_dwconv(full, w, b):
    y = lax.conv_general_dilated(full, w[:, None, :], window_strides=(1,), padding='VALID',
                                 dimension_numbers=('NWC', 'WIO', 'NWC'),
                                 feature_group_count=full.shape[-1])
    return y + b


def ssd_scan(x, dt, a, bm, cm, h0):
    bsz, L = x.shape[:2]
    q = SSD_CHUNK if L % SSD_CHUNK == 0 else L
    nc = L // q
    r = SSD_HEADS // SSD_GROUPS

    def chunks(t, tail):
        return jnp.moveaxis(t.astype(F32).reshape((bsz, nc, q) + tail), 1, 0)

    xs = chunks(x, (SSD_GROUPS, r, SSD_HEAD_DIM))
    dts = chunks(dt, (SSD_GROUPS, r))
    bs = chunks(bm, (SSD_GROUPS, SSD_STATE))
    cs = chunks(cm, (SSD_GROUPS, SSD_STATE))
    ag = a.astype(F32).reshape(SSD_GROUPS, r)
    causal = jnp.tril(jnp.ones((q, q), bool))[None, :, :, None, None]

    def step(h, inp):
        xc, dtc, bc, cc = inp
        cum = jnp.cumsum(dtc * ag, axis=1)
        decay = jnp.exp(jnp.where(causal, cum[:, :, None] - cum[:, None, :], -jnp.inf))
        cb = jnp.einsum('btgn,bsgn->btsg', cc, bc)
        y = jnp.einsum('btsg,btsgr,bsgrp->btgrp', cb, decay, xc * dtc[..., None])
        y = y + jnp.einsum('btgn,bgrpn->btgrp', cc, h) * jnp.exp(cum)[..., None]
        w_end = jnp.exp(cum[:, -1:] - cum) * dtc
        h = h * jnp.exp(cum[:, -1])[..., None, None] + jnp.einsum('bsgr,bsgn,bsgrp->bgrpn', w_end, bc, xc)
        return h, y

    h0g = h0.astype(F32).reshape(bsz, SSD_GROUPS, r, SSD_HEAD_DIM, SSD_STATE)
    hT, ys = lax.scan(step, h0g, (xs, dts, bs, cs))
    y = jnp.moveaxis(ys, 0, 1).reshape(bsz, L, SSD_HEADS, SSD_HEAD_DIM)
    return y.astype(x.dtype), hT.reshape(bsz, SSD_HEADS, SSD_HEAD_DIM, SSD_STATE).astype(h0.dtype)


def hgrn2_scan(q, k, v, logf, s0):
    bsz, L = q.shape[:2]
    c = HG_CHUNK if L % HG_CHUNK == 0 else L
    nc = L // c

    def chunks(t):
        return jnp.moveaxis(t.astype(F32).reshape((bsz, nc, c) + t.shape[2:]), 1, 0)

    causal = jnp.tril(jnp.ones((c, c), bool))[None, :, :, None, None]

    def step(s, inp):
        qc, kc, vc, lc = inp
        g = jnp.cumsum(lc, axis=1)
        decay = jnp.exp(jnp.where(causal, g[:, :, None] - g[:, None, :], -jnp.inf))
        att = jnp.einsum('bthk,btshk,bshk->btsh', qc, decay, kc)
        o = jnp.einsum('btsh,bshv->bthv', att, vc) + jnp.einsum('bthk,bhkv->bthv', qc * jnp.exp(g), s)
        s = s * jnp.exp(g[:, -1])[..., None] + jnp.einsum('bshk,bshv->bhkv', kc * jnp.exp(g[:, -1:] - g), vc)
        return s, o

    sT, os_ = lax.scan(step, s0.astype(F32), (chunks(q), chunks(k), chunks(v), chunks(logf)))
    o = jnp.moveaxis(os_, 0, 1).reshape(bsz, L, HG_HEADS, HG_HEAD_V)
    return o.astype(q.dtype), sT.astype(s0.dtype)


def token_mixers(hn, conv_buf, ssm_h, hg_s, p, lb):
    bsz, L, _ = hn.shape
    proj = hn @ p['w_in']
    z, xbc, dt_raw, hq, hf, hi, hgate, g_ssd, g_hg = jnp.split(proj, IN_SPLITS, axis=-1)
    full = jnp.concatenate([conv_buf, xbc], axis=1)
    conv_new = full[:, -(SSD_CONV - 1):]
    xbc = jax.nn.silu(causal_dwconv(full, p['conv_w'], p['conv_b']))
    xs, bm, cm = jnp.split(xbc, [SSD_INNER, SSD_INNER + SSD_GROUPS * SSD_STATE], axis=-1)
    xs = xs.reshape(bsz, L, SSD_HEADS, SSD_HEAD_DIM)
    bm = bm.reshape(bsz, L, SSD_GROUPS, SSD_STATE)
    cm = cm.reshape(bsz, L, SSD_GROUPS, SSD_STATE)
    dt = jax.nn.softplus(dt_raw.astype(F32) + p['dt_bias'].astype(F32))
    a = -jnp.exp(p['a_log'].astype(F32))
    y, ssm_new = ssd_scan(xs, dt, a, bm, cm, ssm_h)
    y = (y + xs * p['d_skip'][:, None]).reshape(bsz, L, SSD_INNER) * jax.nn.silu(z)
    ssd_out = group_rmsnorm(y, p['ssd_norm'], SSD_GROUPS) @ p['w_ssd_out']
    u = hf.astype(F32)
    logf = jnp.log(lb + (1.0 - lb) * jax.nn.sigmoid(u))
    k = (1.0 - lb) * jax.nn.sigmoid(-u)
    heads = lambda t: t.reshape(bsz, L, HG_HEADS, -1)
    o, hg_new = hgrn2_scan(heads(jax.nn.silu(hq)), heads(k), heads(hi), heads(logf), hg_s)
    o = group_rmsnorm(o.reshape(bsz, L, HG_VDIM), p['hgrn_norm'], HG_HEADS) * jax.nn.silu(hgate)
    hg_out = o @ p['w_hgrn_out']
    merged = jax.nn.sigmoid(g_ssd) * ssd_out + jax.nn.sigmoid(g_hg) * hg_out
    return merged @ p['w_out'], conv_new, ssm_new, hg_new


def memory_kv(mem, p):
    bsz = mem.shape[0]
    mn = rmsnorm(mem, p['norm_mem'])
    mk = (mn @ p['w_ck']).reshape(bsz, MEM_LEN, XA_HEADS, XA_HEAD_DIM)
    mv = (mn @ p['w_cv']).reshape(bsz, MEM_LEN, XA_HEADS, XA_HEAD_DIM)
    return mk, mv


def cross_attn(hn, mk, mv, p):
    bsz, L, _ = hn.shape
    q = (hn @ p['w_cq']).reshape(bsz, L, XA_HEADS, XA_HEAD_DIM)
    s = jnp.einsum('blhd,bmhd->bhlm', q, mk).astype(F32) * (XA_HEAD_DIM ** -0.5)
    pr = jax.nn.softmax(s, axis=-1).astype(mv.dtype)
    o = jnp.einsum('bhlm,bmhd->blhd', pr, mv).reshape(bsz, L, XA_HEADS * XA_HEAD_DIM)
    return o @ p['w_co']


def swiglu(hn, p):
    return (jax.nn.silu(hn @ p['w_gate']) * (hn @ p['w_up'])) @ p['w_down']


def decoder_layer(x, conv_buf, ssm_h, hg_s, mk, mv, p, lb):
    mix, conv_new, ssm_new, hg_new = token_mixers(rmsnorm(x, p['norm_mix']), conv_buf, ssm_h, hg_s, p, lb)
    x = x + mix
    x = x + cross_attn(rmsnorm(x, p['norm_cross']), mk, mv, p)
    x = x + swiglu(rmsnorm(x, p['norm_ffn']), p)
    return x, conv_new, ssm_new, hg_new


def setup_inputs(seed: int = 0) -> dict:
    key = jax.random.key(seed)
    ks = jax.random.split(key, 32)
    nrm = lambda k, shape, scale: jax.random.normal(k, shape, jnp.float32) * scale
    gain = lambda k, shape: 1.0 + 0.02 * jax.random.normal(k, shape, jnp.float32)
    dt0 = jnp.exp(jax.random.uniform(ks[12], (DEPTH, SSD_HEADS), jnp.float32, math.log(1e-3), math.log(1e-1)))
    dt_bias = dt0 + jnp.log(-jnp.expm1(-dt0))
    a_log = jnp.log(jax.random.uniform(ks[13], (DEPTH, SSD_HEADS), jnp.float32, 1.0, 16.0))
    xa = XA_HEADS * XA_HEAD_DIM
    return {
        'x_prompt': nrm(ks[0], (BATCH, SEQ, D_MODEL), 1.0),
        'x_sample': nrm(ks[1], (DEC_BATCH, DEC_SEQ, D_MODEL), 1.0),
        'mem_prompt': nrm(ks[2], (BATCH, MEM_LEN, D_MODEL), 1.0),
        'state_ssm': nrm(ks[3], (DEPTH, DEC_BATCH, SSD_HEADS, SSD_HEAD_DIM, SSD_STATE), 0.5),
        'state_conv': nrm(ks[4], (DEPTH, DEC_BATCH, SSD_CONV - 1, SSD_CONV_DIM), 1.0),
        'state_hgrn': nrm(ks[5], (DEPTH, DEC_BATCH, HG_HEADS, HG_EXPAND, HG_HEAD_V), 0.5),
        'cache_mem_k': nrm(ks[6], (DEPTH, DEC_BATCH, MEM_LEN, XA_HEADS, XA_HEAD_DIM), 1.0),
        'cache_mem_v': nrm(ks[7], (DEPTH, DEC_BATCH, MEM_LEN, XA_HEADS, XA_HEAD_DIM), 1.0),
        'norm_mix': gain(ks[8], (DEPTH, D_MODEL)),
        'w_in': nrm(ks[9], (DEPTH, D_MODEL, IN_DIM), D_MODEL ** -0.5),
        'conv_w': nrm(ks[10], (DEPTH, SSD_CONV, SSD_CONV_DIM), SSD_CONV ** -0.5),
        'conv_b': nrm(ks[11], (DEPTH, SSD_CONV_DIM), 0.02),
        'dt_bias': dt_bias,
        'a_log': a_log,
        'd_skip': gain(ks[14], (DEPTH, SSD_HEADS)),
        'ssd_norm': gain(ks[15], (DEPTH, SSD_INNER)),
        'w_ssd_out': nrm(ks[16], (DEPTH, SSD_INNER, D_MODEL), SSD_INNER ** -0.5),
        'hgrn_lb': nrm(ks[17], (DEPTH + 1, HG_KDIM), 0.5),
        'hgrn_norm': gain(ks[18], (DEPTH, HG_VDIM)),
        'w_hgrn_out': nrm(ks[19], (DEPTH, HG_VDIM, D_MODEL), HG_VDIM ** -0.5),
        'w_out': nrm(ks[20], (DEPTH, D_MODEL, D_MODEL), D_MODEL ** -0.5),
        'norm_cross': gain(ks[21], (DEPTH, D_MODEL)),
        'norm_mem': gain(ks[22], (DEPTH, D_MODEL)),
        'w_cq': nrm(ks[23], (DEPTH, D_MODEL, xa), D_MODEL ** -0.5),
        'w_ck': nrm(ks[24], (DEPTH, D_MODEL, xa), D_MODEL ** -0.5),
        'w_cv': nrm(ks[25], (DEPTH, D_MODEL, xa), D_MODEL ** -0.5),
        'w_co': nrm(ks[26], (DEPTH, xa, D_MODEL), xa ** -0.5),
        'norm_ffn': gain(ks[27], (DEPTH, D_MODEL)),
        'w_gate': nrm(ks[28], (DEPTH, D_MODEL, FFN_DIM), D_MODEL ** -0.5),
        'w_up': nrm(ks[29], (DEPTH, D_MODEL, FFN_DIM), D_MODEL ** -0.5),
        'w_down': nrm(ks[30], (DEPTH, FFN_DIM, D_MODEL), FFN_DIM ** -0.5),
        'norm_final': gain(ks[31], (D_MODEL,)),
    }


def reference(x_prompt, x_sample, mem_prompt, state_ssm, state_conv, state_hgrn, cache_mem_k, cache_mem_v,
              norm_mix, w_in, conv_w, conv_b, dt_bias, a_log, d_skip, ssd_norm, w_ssd_out,
              hgrn_lb, hgrn_norm, w_hgrn_out, w_out, norm_cross, norm_mem, w_cq, w_ck, w_cv, w_co,
              norm_ffn, w_gate, w_up, w_down, norm_final):
    lbs = jnp.cumsum(jax.nn.softmax(hgrn_lb.astype(F32), axis=0), axis=0)
    bp = x_prompt.shape[0]
    dtype = x_prompt.dtype
    conv0 = jnp.zeros((bp, SSD_CONV - 1, SSD_CONV_DIM), dtype)
    ssm0 = jnp.zeros((bp, SSD_HEADS, SSD_HEAD_DIM, SSD_STATE), dtype)
    hg0 = jnp.zeros((bp, HG_HEADS, HG_EXPAND, HG_HEAD_V), dtype)
    hp, hs = x_prompt, x_sample
    ssm_p, conv_p, hg_p, mk_p, mv_p = [], [], [], [], []
    ssm_s, conv_s, hg_s = [], [], []
    for l in range(DEPTH):
        p = dict(norm_mix=norm_mix[l], w_in=w_in[l], conv_w=conv_w[l], conv_b=conv_b[l], dt_bias=dt_bias[l],
                 a_log=a_log[l], d_skip=d_skip[l], ssd_norm=ssd_norm[l], w_ssd_out=w_ssd_out[l],
                 hgrn_norm=hgrn_norm[l], w_hgrn_out=w_hgrn_out[l], w_out=w_out[l], norm_cross=norm_cross[l],
                 norm_mem=norm_mem[l], w_cq=w_cq[l], w_ck=w_ck[l], w_cv=w_cv[l], w_co=w_co[l],
                 norm_ffn=norm_ffn[l], w_gate=w_gate[l], w_up=w_up[l], w_down=w_down[l])
        lb = lbs[l]
        mk, mv = memory_kv(mem_prompt, p)
        hp, c_new, s_new, g_new = decoder_layer(hp, conv0, ssm0, hg0, mk, mv, p, lb)
        ssm_p.append(s_new); conv_p.append(c_new); hg_p.append(g_new); mk_p.append(mk); mv_p.append(mv)
        hs, c_new, s_new, g_new = decoder_layer(hs, state_conv[l], state_ssm[l], state_hgrn[l],
                                                cache_mem_k[l], cache_mem_v[l], p, lb)
        ssm_s.append(s_new); conv_s.append(c_new); hg_s.append(g_new)
    y_prompt = rmsnorm(hp, norm_final)
    y_sample = rmsnorm(hs, norm_final)
    return (y_prompt, y_sample,
            jnp.stack(ssm_p), jnp.stack(conv_p), jnp.stack(hg_p), jnp.stack(mk_p), jnp.stack(mv_p),
            jnp.stack(ssm_s), jnp.stack(conv_s), jnp.stack(hg_s))
```

```python
import functools

import jax
import jax.numpy as jnp
from jax import lax
from jax.experimental import pallas as pl
from jax.experimental.pallas import tpu as pltpu

F32 = jnp.float32
BF16 = jnp.bfloat16

D_MODEL = 1024
SSD_INNER = 2048
SSD_HEAD_DIM = 64
SSD_HEADS = 32
SSD_GROUPS = 4
SSD_HPG = SSD_HEADS // SSD_GROUPS
SSD_STATE = 128
SSD_CONV = 4
SSD_CONV_DIM = SSD_INNER + 2 * SSD_GROUPS * SSD_STATE
SSD_GW = SSD_HPG * SSD_HEAD_DIM
HG_HEADS = 8
HG_K = 128
HG_V = 128
HG_DIM = HG_HEADS * HG_K
MEM_LEN = 256
XA_HEADS = 4
XA_HEAD_DIM = 256
FFN_DIM = 2816
EPS = 1e-6

LANE = 128
SUBLANE = 8
VMEM_LIMIT = 56 * 1024 * 1024

CHUNK = 128
SLAB = SUBLANE
SLAB_TOK0 = 3
SLAB_NTOK = 4
NEG = -1e30

SSD_IN_W = SSD_INNER + SSD_CONV_DIM + LANE
HG_IN_W = 6 * D_MODEL


def _dot(a, b):
    return jnp.dot(a.astype(BF16), b.astype(BF16), preferred_element_type=F32)


def _dot_nt(a, b):
    return lax.dot_general(a.astype(BF16), b.astype(BF16), (((1,), (1,)), ((), ())),
                           preferred_element_type=F32)


def _dot_tn(a, b):
    return lax.dot_general(a.astype(BF16), b.astype(BF16), (((0,), (0,)), ((), ())),
                           preferred_element_type=F32)


def _split2(x):
    hi = x.astype(BF16)
    lo = (x - hi.astype(F32)).astype(BF16)
    return hi, lo


def _split3(x):
    hi = x.astype(BF16)
    r = x - hi.astype(F32)
    mid = r.astype(BF16)
    lo = (r - mid.astype(F32)).astype(BF16)
    return hi, mid, lo


def _sel_dot_l(sel, x):
    hi, mid, lo = _split3(x)
    d = lambda p: jnp.dot(sel, p, preferred_element_type=F32)
    return d(hi) + d(mid) + d(lo)


def _sel_dot_r(x, sel):
    hi, mid, lo = _split3(x)
    d = lambda p: jnp.dot(p, sel, preferred_element_type=F32)
    return d(hi) + d(mid) + d(lo)


def _sigmoid(x):
    return 1.0 / (1.0 + jnp.exp(-x))


def _silu(x):
    return x * _sigmoid(x)


def _softplus(x):
    return jnp.maximum(x, 0.0) + jnp.log(1.0 + jnp.exp(-jnp.abs(x)))


def _rms(x, w):
    return x * lax.rsqrt(jnp.mean(x * x, axis=-1, keepdims=True) + EPS) * w


def _group_rms(x, w, groups):
    gw = x.shape[-1] // groups
    parts = []
    for g in range(groups):
        xg = x[:, g * gw:(g + 1) * gw]
        parts.append(xg * lax.rsqrt(jnp.mean(xg * xg, axis=-1, keepdims=True) + EPS))
    return jnp.concatenate(parts, axis=-1) * w


def _iota(shape, dim):
    return lax.broadcasted_iota(jnp.int32, shape, dim)


def _blk_row(x, blk, row):
    r, w = x.shape
    x3 = x.reshape(r // blk, blk, w)
    return jnp.broadcast_to(x3[:, row:row + 1, :], x3.shape).reshape(r, w)


def _blk_id(i, blk):
    return lax.shift_right_logical(i, blk.bit_length() - 1)


def _causal_mask(n, blk):
    t = _iota((n, n), 0)
    s = _iota((n, n), 1)
    return (s <= t) & (_blk_id(t, blk) == _blk_id(s, blk))


def _cumsum_matrix(n, blk):
    return _causal_mask(n, blk).astype(BF16)


def _head_expand_matrix():
    r = _iota((LANE, SSD_INNER), 0)
    c = _iota((LANE, SSD_INNER), 1)
    return (_blk_id(c, SSD_HEAD_DIM) == r).astype(BF16)


def _ssd_terms(dt_raw, dt_bias, a_log, csum, last_of, valid=None):
    dt = _softplus(dt_raw + dt_bias)
    if valid is not None:
        dt = jnp.where(valid, dt, 0.0)
    da = dt * (-jnp.exp(a_log))
    cum = _sel_dot_l(csum, da)
    cum_last = last_of(cum)
    return dt, cum, jnp.exp(cum), jnp.exp(cum_last - cum) * dt, jnp.exp(cum_last)


def _ssd_intra(cum, cb_of_group, xdt, mask):
    n = cum.shape[0]
    cum_t = cum.T
    lane = _iota((n, LANE), 1)
    outs = []
    for g in range(SSD_GROUPS):
        cb = cb_of_group(g)
        for pair in range(SSD_HPG // 2):
            ms = []
            for j in range(2):
                h = g * SSD_HPG + 2 * pair + j
                diff = cum[:, h:h + 1] - cum_t[h:h + 1, :]
                ms.append((cb * jnp.exp(jnp.where(mask, diff, NEG))).astype(BF16))
            col = (g * SSD_HPG + 2 * pair) * SSD_HEAD_DIM
            xp = xdt[:, col:col + LANE]
            rhs = jnp.concatenate([jnp.where(lane < SSD_HEAD_DIM, xp, 0.0),
                                   jnp.where(lane >= SSD_HEAD_DIM, xp, 0.0)], axis=0).astype(BF16)
            outs.append(jnp.dot(jnp.concatenate(ms, axis=1), rhs, preferred_element_type=F32))
    return jnp.concatenate(outs, axis=1)


def _conv_silu(win_ref, n, conv_w, conv_b):
    acc = conv_b
    for d in range(SSD_CONV):
        acc = acc + win_ref[pl.ds(SUBLANE - d, n), :] * conv_w[SSD_CONV - 1 - d:SSD_CONV - d, :]
    return _silu(acc)


def _hgrn_gates(hq, hf, lb):
    e = jnp.exp(-jnp.abs(hf))
    r = 1.0 / (1.0 + e)
    er = e * r
    pos = hf >= 0.0
    sig = jnp.where(pos, r, er)
    nsig = jnp.where(pos, er, r)
    return _silu(hq), (1.0 - lb) * nsig, jnp.log(lb + (1.0 - lb) * sig)


def _lower_bound(lb_raw):
    m = jnp.max(lb_raw, axis=0, keepdims=True)
    e = jnp.exp(lb_raw - m)
    return e[0:1, :] / jnp.sum(e, axis=0, keepdims=True)


def _hgrn_diag(q, k, v, g):
    n, w = q.shape
    nb = n // SUBLANE
    q3, k3, v3, g3 = (a.reshape(nb, SUBLANE, w) for a in (q, k, v, g))
    tt = _iota((nb, SUBLANE, w), 1)
    o3 = jnp.zeros((nb, SUBLANE, w), F32)
    for s in range(SUBLANE):
        dec = jnp.exp(jnp.where(tt >= s, g3 - g3[:, s:s + 1, :], NEG))
        a = jnp.sum(q3 * dec * k3[:, s:s + 1, :], axis=-1, keepdims=True)
        o3 = o3 + a * v3[:, s:s + 1, :]
    return o3.reshape(n, w)


def _hgrn_cross(q, k, g):
    n = q.shape[0]
    row = _iota(q.shape, 0)
    ti = _iota((n, n), 0)
    si = _iota((n, n), 1)
    a = jnp.zeros((n, n), F32)
    c = SUBLANE
    while c < n:
        c2 = 2 * c
        gm = _blk_row(g, c2, c - 1)
        second = (row & (c2 - 1)) >= c
        z = (jnp.where(second, q, k) * jnp.exp(jnp.where(second, g - gm, gm - g))).astype(BF16)
        m = (_blk_id(ti, c2) == _blk_id(si, c2)) & ((ti & (c2 - 1)) >= c) & ((si & (c2 - 1)) < c)
        a = jnp.where(m, _dot_nt(z, z), a)
        c = c2
    return a


def _rms_matmul_kernel(x_ref, nw_ref, w_ref, o_ref):
    o_ref[...] = _dot(_rms(x_ref[...], nw_ref[...]), w_ref[...])


def _rms_matmul(x, norm_w, w, tm, name):
    m, k = x.shape
    n = w.shape[1]
    return pl.pallas_call(
        _rms_matmul_kernel,
        out_shape=jax.ShapeDtypeStruct((m, n), F32),
        grid=(m // tm,),
        in_specs=[pl.BlockSpec((tm, k), lambda i: (i, 0)),
                  pl.BlockSpec((1, k), lambda i: (0, 0)),
                  pl.BlockSpec((k, n), lambda i: (0, 0))],
        out_specs=pl.BlockSpec((tm, n), lambda i: (i, 0)),
        compiler_params=pltpu.CompilerParams(dimension_semantics=("arbitrary",),
                                             vmem_limit_bytes=VMEM_LIMIT),
        name=name,
    )(x, norm_w, w)


def _ssd_prompt_kernel(x_ref, nw_ref, win_ref, convw_ref, convb_ref, dtb_ref, alog_ref, dskip_ref,
                       snorm_ref, wso_ref, out_ref, ssm_ref, conv_ref, ht_ref, xwin_ref):
    c = pl.program_id(1)
    n = CHUNK

    @pl.when(c == 0)
    def _():
        ht_ref[...] = jnp.zeros_like(ht_ref)
        xwin_ref[0:SUBLANE, :] = jnp.zeros((SUBLANE, SSD_CONV_DIM), F32)

    xn = _rms(x_ref[0], nw_ref[...]).astype(BF16)
    z = jnp.dot(xn, win_ref[:, 0:SSD_INNER], preferred_element_type=F32)
    xbc = jnp.dot(xn, win_ref[:, SSD_INNER:SSD_INNER + SSD_CONV_DIM], preferred_element_type=F32)
    dt_raw = jnp.dot(xn, win_ref[:, SSD_INNER + SSD_CONV_DIM:SSD_IN_W], preferred_element_type=F32)

    xwin_ref[SUBLANE:SUBLANE + n, :] = xbc
    xa = _conv_silu(xwin_ref, n, convw_ref[...], convb_ref[...])
    xwin_ref[0:SUBLANE, :] = xbc[n - SUBLANE:n, :]
    xs = xa[:, 0:SSD_INNER]
    bm = xa[:, SSD_INNER:SSD_INNER + SSD_GROUPS * SSD_STATE]
    cm = xa[:, SSD_INNER + SSD_GROUPS * SSD_STATE:SSD_CONV_DIM]

    csum = _cumsum_matrix(n, n)
    dt, cum, ecum, wend, dec = _ssd_terms(dt_raw, dtb_ref[...], alog_ref[...], csum,
                                          lambda a: a[n - 1:n, :])
    expand = _head_expand_matrix()
    dt_x = _sel_dot_r(dt, expand)
    ecum_x = _sel_dot_r(ecum, expand)
    wend_x = _sel_dot_r(wend, expand)
    dec_x = _sel_dot_r(jnp.broadcast_to(dec, (SUBLANE, LANE)), expand)[0:1, :]

    cgs = [cm[:, g * SSD_STATE:(g + 1) * SSD_STATE].astype(BF16) for g in range(SSD_GROUPS)]
    bgs = [bm[:, g * SSD_STATE:(g + 1) * SSD_STATE].astype(BF16) for g in range(SSD_GROUPS)]
    y = _ssd_intra(cum, lambda g: _dot_nt(cgs[g], bgs[g]), xs * dt_x, _causal_mask(n, n))

    xw = xs * wend_x
    inter = []
    for g in range(SSD_GROUPS):
        sl = slice(g * SSD_GW, (g + 1) * SSD_GW)
        ht = ht_ref[g]
        inter.append(_dot(cgs[g], ht))
        ht_ref[g] = ht * dec_x[:, sl] + _dot_tn(bgs[g], xw[:, sl])
    y = y + jnp.concatenate(inter, axis=1) * ecum_x + xs * dskip_ref[...]
    y = y * _silu(z)
    out_ref[0] = _dot(_group_rms(y, snorm_ref[...], SSD_GROUPS), wso_ref[...])

    @pl.when(c == pl.num_programs(1) - 1)
    def _():
        for g in range(SSD_GROUPS):
            ssm_ref[0, g * SSD_GW:(g + 1) * SSD_GW, :] = ht_ref[g].T
        conv_ref[0] = xwin_ref[0:SUBLANE, :]


def _ssd_prompt(x, norm_mix, w_ssd_in, conv_w, conv_b, dt_bias, a_log, d_skip_x, ssd_norm, w_ssd_out):
    b, l, _ = x.shape
    const = lambda shape: pl.BlockSpec(shape, lambda i, j: (0,) * len(shape))
    return pl.pallas_call(
        _ssd_prompt_kernel,
        out_shape=(jax.ShapeDtypeStruct((b, l, D_MODEL), F32),
                   jax.ShapeDtypeStruct((b, SSD_INNER, SSD_STATE), F32),
                   jax.ShapeDtypeStruct((b, SUBLANE, SSD_CONV_DIM), F32)),
        grid=(b, l // CHUNK),
        in_specs=[pl.BlockSpec((1, CHUNK, D_MODEL), lambda i, j: (i, j, 0)),
                  const((1, D_MODEL)), const((D_MODEL, SSD_IN_W)),
                  const((SSD_CONV, SSD_CONV_DIM)), const((1, SSD_CONV_DIM)),
                  const((1, LANE)), const((1, LANE)), const((1, SSD_INNER)), const((1, SSD_INNER)),
                  const((SSD_INNER, D_MODEL))],
        out_specs=(pl.BlockSpec((1, CHUNK, D_MODEL), lambda i, j: (i, j, 0)),
                   pl.BlockSpec((1, SSD_INNER, SSD_STATE), lambda i, j: (i, 0, 0)),
                   pl.BlockSpec((1, SUBLANE, SSD_CONV_DIM), lambda i, j: (i, 0, 0))),
        scratch_shapes=[pltpu.VMEM((SSD_GROUPS, SSD_STATE, SSD_GW), F32),
                        pltpu.VMEM((SUBLANE + CHUNK, SSD_CONV_DIM), F32)],
        compiler_params=pltpu.CompilerParams(dimension_semantics=("arbitrary", "arbitrary"),
                                             vmem_limit_bytes=VMEM_LIMIT),
        name="ssd_prompt",
    )(x, norm_mix, w_ssd_in, conv_w, conv_b, dt_bias, a_log, d_skip_x, ssd_norm, w_ssd_out)


def _merge_out(x, ssd_out, hg_pre, g_ssd, g_hg, whg_ref, wout_ref):
    hg_out = _dot(hg_pre, whg_ref[...])
    merged = _sigmoid(g_ssd) * ssd_out + _sigmoid(g_hg) * hg_out
    return x + _dot(merged, wout_ref[...])


def _hgrn_prompt_kernel(x_ref, ssd_ref, nw_ref, win_ref, lb_ref, hnorm_ref, whg_ref, wout_ref,
                        out_ref, hst_ref, st_ref):
    c = pl.program_id(1)
    n = CHUNK

    @pl.when(c == 0)
    def _():
        st_ref[...] = jnp.zeros_like(st_ref)

    x = x_ref[0]
    xn = _rms(x, nw_ref[...]).astype(BF16)
    seg = lambda i: jnp.dot(xn, win_ref[:, i * D_MODEL:(i + 1) * D_MODEL], preferred_element_type=F32)
    lb = _lower_bound(lb_ref[...])
    q, k, logf = _hgrn_gates(seg(0), seg(1), lb)
    v = seg(2)
    g = _sel_dot_l(_cumsum_matrix(n, n), logf)
    outs = []
    for h in range(HG_HEADS):
        sl = slice(h * HG_K, (h + 1) * HG_K)
        qh, kh, vh, gh = q[:, sl], k[:, sl], v[:, sl], g[:, sl]
        g_last = gh[n - 1:n, :]
        st = st_ref[h]
        o = _hgrn_diag(qh, kh, vh, gh) + _dot(_hgrn_cross(qh, kh, gh), vh)
        o = o + _dot_nt(qh * jnp.exp(gh), st)
        st_ref[h] = st * jnp.exp(g_last) + _dot_tn(vh, kh * jnp.exp(g_last - gh))
        outs.append(o)
    o = _group_rms(jnp.concatenate(outs, axis=1), hnorm_ref[...], HG_HEADS) * _silu(seg(3))
    out_ref[0] = _merge_out(x, ssd_ref[0], o, seg(4), seg(5), whg_ref, wout_ref)

    @pl.when(c == pl.num_programs(1) - 1)
    def _():
        for h in range(HG_HEADS):
            hst_ref[0, h * HG_K:(h + 1) * HG_K, :] = st_ref[h].T


def _hgrn_prompt(x, ssd_out, norm_mix, w_hg_in, hgrn_lb, hgrn_norm, w_hgrn_out, w_out):
    b, l, _ = x.shape
    const = lambda shape: pl.BlockSpec(shape, lambda i, j: (0,) * len(shape))
    tok = pl.BlockSpec((1, CHUNK, D_MODEL), lambda i, j: (i, j, 0))
    return pl.pallas_call(
        _hgrn_prompt_kernel,
        out_shape=(jax.ShapeDtypeStruct((b, l, D_MODEL), F32),
                   jax.ShapeDtypeStruct((b, HG_DIM, HG_V), F32)),
        grid=(b, l // CHUNK),
        in_specs=[tok, tok, const((1, D_MODEL)), const((D_MODEL, HG_IN_W)), const((2, HG_DIM)),
                  const((1, HG_DIM)), const((HG_DIM, D_MODEL)), const((D_MODEL, D_MODEL))],
        out_specs=(tok, pl.BlockSpec((1, HG_DIM, HG_V), lambda i, j: (i, 0, 0))),
        scratch_shapes=[pltpu.VMEM((HG_HEADS, HG_V, HG_K), F32)],
        compiler_params=pltpu.CompilerParams(dimension_semantics=("arbitrary", "arbitrary"),
                                             vmem_limit_bytes=VMEM_LIMIT),
        name="hgrn_prompt",
    )(x, ssd_out, norm_mix, w_hg_in, hgrn_lb, hgrn_norm, w_hgrn_out, w_out)


def _attend(q, kmat, vmat):
    outs = []
    for h in range(XA_HEADS):
        sl = slice(h * XA_HEAD_DIM, (h + 1) * XA_HEAD_DIM)
        s = _dot_nt(q[:, sl], kmat[:, sl]) * (XA_HEAD_DIM ** -0.5)
        p = jnp.exp(s - jnp.max(s, axis=-1, keepdims=True))
        p = p / jnp.sum(p, axis=-1, keepdims=True)
        outs.append(_dot(p, vmat[:, sl]))
    return jnp.concatenate(outs, axis=1)


def _attn_prompt_kernel(h_ref, nw_ref, wq_ref, kv_ref, o_ref):
    q = _dot(_rms(h_ref[0], nw_ref[...]), wq_ref[...])
    o_ref[0] = _attend(q, kv_ref[0, :, 0:D_MODEL], kv_ref[0, :, D_MODEL:2 * D_MODEL])


def _attn_prompt(h1, norm_cross, w_cq, kv, tq):
    b, l, _ = h1.shape
    tok = pl.BlockSpec((1, tq, D_MODEL), lambda i, j: (i, j, 0))
    return pl.pallas_call(
        _attn_prompt_kernel,
        out_shape=jax.ShapeDtypeStruct((b, l, D_MODEL), F32),
        grid=(b, l // tq),
        in_specs=[tok, pl.BlockSpec((1, D_MODEL), lambda i, j: (0, 0)),
                  pl.BlockSpec((D_MODEL, D_MODEL), lambda i, j: (0, 0)),
                  pl.BlockSpec((1, MEM_LEN, 2 * D_MODEL), lambda i, j: (i, 0, 0))],
        out_specs=tok,
        compiler_params=pltpu.CompilerParams(dimension_semantics=("arbitrary", "arbitrary"),
                                             vmem_limit_bytes=VMEM_LIMIT),
        name="attn_prompt",
    )(h1, norm_cross, w_cq, kv)


def _attn_sample_kernel(q_ref, k_ref, v_ref, o_ref):
    for i in range(k_ref.shape[0]):
        rows = slice(i * SLAB, (i + 1) * SLAB)
        o_ref[rows, :] = _attend(q_ref[rows, :], k_ref[i], v_ref[i])


def _attn_sample(q, cache_k, cache_v, nb):
    b = cache_k.shape[0]
    kvspec = pl.BlockSpec((nb, MEM_LEN, D_MODEL), lambda i: (i, 0, 0))
    tok = pl.BlockSpec((nb * SLAB, D_MODEL), lambda i: (i, 0))
    return pl.pallas_call(
        _attn_sample_kernel,
        out_shape=jax.ShapeDtypeStruct((b * SLAB, D_MODEL), F32),
        grid=(b // nb,),
        in_specs=[tok, kvspec, kvspec],
        out_specs=tok,
        compiler_params=pltpu.CompilerParams(dimension_semantics=("arbitrary",),
                                             vmem_limit_bytes=VMEM_LIMIT),
        name="attn_sample",
    )(q, cache_k, cache_v)


def _co_ffn_kernel(h_ref, a_ref, wco_ref, nffn_ref, wg_ref, wu_ref, wd_ref, nfin_ref, o_ref):
    h = h_ref[...] + _dot(a_ref[...], wco_ref[...])
    hn = _rms(h, nffn_ref[...]).astype(BF16)
    gate = jnp.dot(hn, wg_ref[...], preferred_element_type=F32)
    up = jnp.dot(hn, wu_ref[...], preferred_element_type=F32)
    h = h + _dot(_silu(gate) * up, wd_ref[...])
    o_ref[...] = _rms(h, nfin_ref[...])


def _co_ffn(h1, attn, w_co, norm_ffn, w_gate, w_up, w_down, norm_final, tm, name):
    m = h1.shape[0]
    tok = pl.BlockSpec((tm, D_MODEL), lambda i: (i, 0))
    const = lambda shape: pl.BlockSpec(shape, lambda i: (0, 0), pipeline_mode=pl.Buffered(1))
    return pl.pallas_call(
        _co_ffn_kernel,
        out_shape=jax.ShapeDtypeStruct((m, D_MODEL), F32),
        grid=(m // tm,),
        in_specs=[tok, tok, const((D_MODEL, D_MODEL)), const((1, D_MODEL)),
                  const((D_MODEL, FFN_DIM)), const((D_MODEL, FFN_DIM)), const((FFN_DIM, D_MODEL)),
                  const((1, D_MODEL))],
        out_specs=tok,
        compiler_params=pltpu.CompilerParams(dimension_semantics=("arbitrary",),
                                             vmem_limit_bytes=VMEM_LIMIT),
        name=name,
    )(h1, attn, w_co, norm_ffn, w_gate, w_up, w_down, norm_final)


def _sample_pre_kernel(ps_ref, ph_ref, cst_ref, convw_ref, convb_ref, dtb_ref, alog_ref, dskip_ref, lb_ref,
                       ycore_ref, ecum_ref, dec_ref, cm_ref, bm_ref, xw_ref, conv_ref,
                       odiag_ref, qg_ref, kg_ref, v_ref, sdec_ref, xwin_ref):
    n = ps_ref.shape[0]
    row = _iota((n, LANE), 0) & (SLAB - 1)
    valid = (row >= SLAB_TOK0) & (row < SLAB_TOK0 + SLAB_NTOK)

    full = cst_ref[...] + ps_ref[:, SSD_INNER:SSD_INNER + SSD_CONV_DIM]
    conv_ref[...] = full
    xwin_ref[0:SUBLANE, :] = jnp.zeros((SUBLANE, SSD_CONV_DIM), F32)
    xwin_ref[SUBLANE:SUBLANE + n, :] = full
    xa = _conv_silu(xwin_ref, n, convw_ref[...], convb_ref[...])
    xs = xa[:, 0:SSD_INNER]
    bm = xa[:, SSD_INNER:SSD_INNER + SSD_GROUPS * SSD_STATE]
    cm = xa[:, SSD_INNER + SSD_GROUPS * SSD_STATE:SSD_CONV_DIM]

    csum = _cumsum_matrix(n, SLAB)
    dt, cum, ecum, wend, dec = _ssd_terms(ps_ref[:, SSD_INNER + SSD_CONV_DIM:SSD_IN_W], dtb_ref[...],
                                          alog_ref[...], csum, lambda a: _blk_row(a, SLAB, SLAB - 1), valid)
    expand = _head_expand_matrix()
    cgs = [cm[:, g * SSD_STATE:(g + 1) * SSD_STATE] for g in range(SSD_GROUPS)]
    bgs = [bm[:, g * SSD_STATE:(g + 1) * SSD_STATE] for g in range(SSD_GROUPS)]
    y = _ssd_intra(cum, lambda g: _dot_nt(cgs[g], bgs[g]), xs * _sel_dot_r(dt, expand), _causal_mask(n, SLAB))
    ycore_ref[...] = y + xs * dskip_ref[...]
    ecum_ref[...] = ecum
    dec_ref[...] = _sel_dot_r(dec, expand)
    cm_ref[...] = cm
    bm_ref[...] = bm
    xw_ref[...] = xs * _sel_dot_r(wend, expand)

    lb = _lower_bound(lb_ref[...])
    q, k, logf = _hgrn_gates(ph_ref[:, 0:D_MODEL], ph_ref[:, D_MODEL:2 * D_MODEL], lb)
    v = ph_ref[:, 2 * D_MODEL:3 * D_MODEL]
    roww = _iota((n, HG_DIM), 0) & (SLAB - 1)
    validw = (roww >= SLAB_TOK0) & (roww < SLAB_TOK0 + SLAB_NTOK)
    k = jnp.where(validw, k, 0.0)
    logf = jnp.where(validw, logf, 0.0)
    g = _sel_dot_l(csum, logf)
    g_last = _blk_row(g, SLAB, SLAB - 1)
    for h in range(HG_HEADS):
        sl = slice(h * HG_K, (h + 1) * HG_K)
        odiag_ref[:, sl] = _hgrn_diag(q[:, sl], k[:, sl], v[:, sl], g[:, sl])
    qg_ref[...] = q * jnp.exp(g)
    kg_ref[...] = k * jnp.exp(g_last - g)
    v_ref[...] = v
    sdec_ref[...] = jnp.exp(g_last)


def _sample_pre(proj_ssd, proj_hg, conv_state, conv_w, conv_b, dt_bias, a_log, d_skip_x, hgrn_lb, tm):
    m = proj_ssd.shape[0]
    rows = lambda w: pl.BlockSpec((tm, w), lambda i: (i, 0))
    const = lambda shape: pl.BlockSpec(shape, lambda i: (0, 0))
    widths = (SSD_INNER, LANE, SSD_INNER, SSD_GROUPS * SSD_STATE, SSD_GROUPS * SSD_STATE, SSD_INNER,
              SSD_CONV_DIM, HG_DIM, HG_DIM, HG_DIM, HG_DIM, HG_DIM)
    return pl.pallas_call(
        _sample_pre_kernel,
        out_shape=tuple(jax.ShapeDtypeStruct((m, w), F32) for w in widths),
        grid=(m // tm,),
        in_specs=[rows(SSD_IN_W), rows(HG_IN_W), rows(SSD_CONV_DIM),
                  const((SSD_CONV, SSD_CONV_DIM)), const((1, SSD_CONV_DIM)), const((1, LANE)), const((1, LANE)),
                  const((1, SSD_INNER)), const((2, HG_DIM))],
        out_specs=tuple(rows(w) for w in widths),
        scratch_shapes=[pltpu.VMEM((SUBLANE + tm, SSD_CONV_DIM), F32)],
        compiler_params=pltpu.CompilerParams(dimension_semantics=("arbitrary",),
                                             vmem_limit_bytes=VMEM_LIMIT),
        name="sample_pre",
    )(proj_ssd, proj_hg, conv_state, conv_w, conv_b, dt_bias, a_log, d_skip_x, hgrn_lb)


def _decay_rows(dec):
    hi, mid, lo = (p.astype(F32) for p in _split3(dec[0:1, :]))
    r = _iota((SLAB, dec.shape[1]), 0)
    return jnp.where(r == 0, hi, jnp.where(r == 1, mid, jnp.where(r == 2, lo, 0.0))).astype(BF16)


def _state_step(state, upd_rows, upd_cols, dec):
    qn = upd_cols.shape[1]
    lhs = jnp.concatenate([upd_rows.astype(BF16), _decay_rows(dec)], axis=0)
    ones3 = (_iota((SLAB, qn), 0) < 3).astype(BF16)
    zeros = jnp.zeros((SLAB, qn), BF16)
    rhs = jnp.concatenate([jnp.concatenate([upd_cols.astype(BF16), zeros], axis=1),
                           jnp.concatenate([zeros, ones3], axis=1)], axis=0)
    both = lax.dot_general(lhs, rhs, (((0,), (0,)), ((), ())), preferred_element_type=F32)
    return state * both[:, qn:2 * qn] + both[:, 0:qn]


def _sample_state_kernel(cm_ref, bm_ref, xw_ref, dec_ref, qg_ref, kg_ref, v_ref, sdec_ref, ssm_ref, hst_ref,
                         yint_ref, oint_ref, ssm_out_ref, hst_out_ref):
    for i in range(ssm_ref.shape[0]):
        rows = slice(i * SLAB, (i + 1) * SLAB)
        for g in range(SSD_GROUPS):
            hsl = slice(g * SSD_GW, (g + 1) * SSD_GW)
            nsl = slice(g * SSD_STATE, (g + 1) * SSD_STATE)
            st = ssm_ref[i, hsl, :]
            yint_ref[rows, hsl] = _dot_nt(cm_ref[rows, nsl], st)
            ssm_out_ref[i, hsl, :] = _state_step(st, xw_ref[rows, hsl], bm_ref[rows, nsl], dec_ref[rows, hsl])
        for h in range(HG_HEADS):
            sl = slice(h * HG_K, (h + 1) * HG_K)
            st = hst_ref[i, sl, :]
            oint_ref[rows, sl] = _dot(qg_ref[rows, sl], st)
            hst_out_ref[i, sl, :] = _state_step(st, kg_ref[rows, sl], v_ref[rows, sl], sdec_ref[rows, sl])


def _sample_state(cm, bm, xw, dec, qg, kg, v, sdec, ssm, hst, nb):
    b = ssm.shape[0]
    rows = lambda w: pl.BlockSpec((nb * SLAB, w), lambda i: (i, 0))
    ssm_spec = pl.BlockSpec((nb, SSD_INNER, SSD_STATE), lambda i: (i, 0, 0))
    hst_spec = pl.BlockSpec((nb, HG_DIM, HG_V), lambda i: (i, 0, 0))
    return pl.pallas_call(
        _sample_state_kernel,
        out_shape=(jax.ShapeDtypeStruct((b * SLAB, SSD_INNER), F32),
                   jax.ShapeDtypeStruct((b * SLAB, HG_DIM), F32),
                   jax.ShapeDtypeStruct(ssm.shape, F32),
                   jax.ShapeDtypeStruct(hst.shape, F32)),
        grid=(b // nb,),
        in_specs=[rows(SSD_GROUPS * SSD_STATE), rows(SSD_GROUPS * SSD_STATE), rows(SSD_INNER), rows(SSD_INNER),
                  rows(HG_DIM), rows(HG_DIM), rows(HG_DIM), rows(HG_DIM), ssm_spec, hst_spec],
        out_specs=(rows(SSD_INNER), rows(HG_DIM), ssm_spec, hst_spec),
        compiler_params=pltpu.CompilerParams(dimension_semantics=("arbitrary",),
                                             vmem_limit_bytes=VMEM_LIMIT),
        name="sample_state",
    )(cm, bm, xw, dec, qg, kg, v, sdec, ssm, hst)


def _sample_merge_kernel(x_ref, z_ref, gates_ref, ycore_ref, yint_ref, ecum_ref, odiag_ref, oint_ref,
                         snorm_ref, wso_ref, hnorm_ref, whg_ref, wout_ref, ncross_ref, wq_ref,
                         h_ref, q_ref):
    y = ycore_ref[...] + yint_ref[...] * _sel_dot_r(ecum_ref[...], _head_expand_matrix())
    y = y * _silu(z_ref[...])
    ssd_out = _dot(_group_rms(y, snorm_ref[...], SSD_GROUPS), wso_ref[...])
    o = _group_rms(odiag_ref[...] + oint_ref[...], hnorm_ref[...], HG_HEADS) * _silu(gates_ref[:, 0:D_MODEL])
    h = _merge_out(x_ref[...], ssd_out, o, gates_ref[:, D_MODEL:2 * D_MODEL], gates_ref[:, 2 * D_MODEL:3 * D_MODEL],
                   whg_ref, wout_ref)
    h_ref[...] = h
    q_ref[...] = _dot(_rms(h, ncross_ref[...]), wq_ref[...])


def _sample_merge(x, proj_ssd, proj_hg, ycore, yint, ecum, odiag, oint, ssd_norm, w_ssd_out, hgrn_norm,
                  w_hgrn_out, w_out, norm_cross, w_cq, tm):
    m = x.shape[0]
    rows = lambda w: pl.BlockSpec((tm, w), lambda i: (i, 0))
    const = lambda shape: pl.BlockSpec(shape, lambda i: (0, 0))
    return pl.pallas_call(
        _sample_merge_kernel,
        out_shape=(jax.ShapeDtypeStruct((m, D_MODEL), F32), jax.ShapeDtypeStruct((m, D_MODEL), F32)),
        grid=(m // tm,),
        in_specs=[rows(D_MODEL),
                  pl.BlockSpec((tm, SSD_INNER), lambda i: (i, 0)),
                  pl.BlockSpec((tm, 3 * D_MODEL), lambda i: (i, 1)),
                  rows(SSD_INNER), rows(SSD_INNER), rows(LANE), rows(HG_DIM), rows(HG_DIM),
                  const((1, SSD_INNER)), const((SSD_INNER, D_MODEL)), const((1, HG_DIM)),
                  const((HG_DIM, D_MODEL)), const((D_MODEL, D_MODEL)), const((1, D_MODEL)),
                  const((D_MODEL, D_MODEL))],
        out_specs=(rows(D_MODEL), rows(D_MODEL)),
        compiler_params=pltpu.CompilerParams(dimension_semantics=("arbitrary",),
                                             vmem_limit_bytes=VMEM_LIMIT),
        name="sample_merge",
    )(x, proj_ssd, proj_hg, ycore, yint, ecum, odiag, oint, ssd_norm, w_ssd_out, hgrn_norm, w_hgrn_out, w_out,
      norm_cross, w_cq)


def kernel(x_prompt, x_sample, mem_prompt, state_ssm, state_conv, state_hgrn, cache_mem_k, cache_mem_v,
           norm_mix, w_in, conv_w, conv_b, dt_bias, a_log, d_skip, ssd_norm, w_ssd_out, hgrn_lb, hgrn_norm,
           w_hgrn_out, w_out, norm_cross, norm_mem, w_cq, w_ck, w_cv, w_co, norm_ffn, w_gate, w_up, w_down,
           norm_final):
    assert norm_mix.shape[0] == 1, "single-layer configuration only"
    bp, lp, _ = x_prompt.shape
    bs, ls, _ = x_sample.shape
    assert ls == SLAB_NTOK and lp % CHUNK == 0

    row = lambda p: p[0].reshape(1, -1)
    pad_lane = lambda p: jnp.pad(p, ((0, 0), (0, LANE - p.shape[1])))
    w_in0 = w_in[0]
    n_ssd = SSD_INNER + SSD_CONV_DIM
    w_ssd_in = jnp.concatenate([w_in0[:, :n_ssd], pad_lane(w_in0[:, n_ssd:n_ssd + SSD_HEADS])], axis=1).astype(BF16)
    w_hg_in = w_in0[:, n_ssd + SSD_HEADS:].astype(BF16)
    dtb = pad_lane(row(dt_bias))
    alog = pad_lane(row(a_log))
    d_skip_x = jnp.repeat(d_skip[0], SSD_HEAD_DIM).reshape(1, SSD_INNER)
    bf = lambda p: p[0].astype(BF16)
    w_so, w_ho, w_o, w_q, w_c = bf(w_ssd_out), bf(w_hgrn_out), bf(w_out), bf(w_cq), bf(w_co)
    w_g, w_u, w_d = bf(w_gate), bf(w_up), bf(w_down)
    w_kv = jnp.concatenate([w_ck[0], w_cv[0]], axis=1).astype(BF16)
    nfin = norm_final.reshape(1, D_MODEL)

    kv = _rms_matmul(mem_prompt.reshape(bp * MEM_LEN, D_MODEL), row(norm_mem), w_kv, 256, "memory_kv")
    ssd_out, ssm_p, conv_p = _ssd_prompt(x_prompt, row(norm_mix), w_ssd_in, conv_w[0], row(conv_b), dtb, alog,
                                         d_skip_x, row(ssd_norm), w_so)
    h1_p, hst_p = _hgrn_prompt(x_prompt, ssd_out, row(norm_mix), w_hg_in, hgrn_lb, row(hgrn_norm), w_ho, w_o)
    att_p = _attn_prompt(h1_p, row(norm_cross), w_q, kv.reshape(bp, MEM_LEN, 2 * D_MODEL), 256)
    y_p = _co_ffn(h1_p.reshape(bp * lp, D_MODEL), att_p.reshape(bp * lp, D_MODEL), w_c, row(norm_ffn),
                  w_g, w_u, w_d, nfin, 256, "co_ffn_prompt")

    ms = bs * SLAB
    xs = jnp.pad(x_sample, ((0, 0), (SLAB_TOK0, SLAB - SLAB_TOK0 - ls), (0, 0))).reshape(ms, D_MODEL)
    cst = jnp.pad(state_conv[0], ((0, 0), (0, SLAB - (SSD_CONV - 1)), (0, 0))).reshape(ms, SSD_CONV_DIM)
    proj_ssd = _rms_matmul(xs, row(norm_mix), w_ssd_in, 256, "sample_in_ssd")
    proj_hg = _rms_matmul(xs, row(norm_mix), w_hg_in, 256, "sample_in_hgrn")
    (ycore, ecum, dec, cm, bm, xw, conv_full, odiag, qg, kg, vv, sdec) = _sample_pre(
        proj_ssd, proj_hg, cst, conv_w[0], row(conv_b), dtb, alog, d_skip_x, hgrn_lb, 128)
    yint, oint, ssm_s, hst_s = _sample_state(
        cm, bm, xw, dec, qg, kg, vv, sdec, state_ssm[0].reshape(bs, SSD_INNER, SSD_STATE),
        state_hgrn[0].reshape(bs, HG_DIM, HG_V), 4)
    h1_s, q_s = _sample_merge(xs, proj_ssd, proj_hg, ycore, yint, ecum, odiag, oint, row(ssd_norm), w_so,
                              row(hgrn_norm), w_ho, w_o, row(norm_cross), w_q, 256)
    att_s = _attn_sample(q_s, cache_mem_k[0].reshape(bs, MEM_LEN, D_MODEL),
                         cache_mem_v[0].reshape(bs, MEM_LEN, D_MODEL), 4)
    y_s = _co_ffn(h1_s, att_s, w_c, row(norm_ffn), w_g, w_u, w_d, nfin, 256, "co_ffn_sample")

    tok = slice(SLAB_TOK0, SLAB_TOK0 + ls)
    kv4 = kv.reshape(bp, MEM_LEN, 2, XA_HEADS, XA_HEAD_DIM)
    return (y_p.reshape(bp, lp, D_MODEL),
            y_s.reshape(bs, SLAB, D_MODEL)[:, tok],
            ssm_p.reshape(1, bp, SSD_HEADS, SSD_HEAD_DIM, SSD_STATE),
            conv_p[None, :, SUBLANE - (SSD_CONV - 1):],
            hst_p.reshape(1, bp, HG_HEADS, HG_K, HG_V),
            kv4[:, :, 0][None],
            kv4[:, :, 1][None],
            ssm_s.reshape(1, bs, SSD_HEADS, SSD_HEAD_DIM, SSD_STATE),
            conv_full.reshape(bs, SLAB, SSD_CONV_DIM)[None, :, SLAB_TOK0 + ls - (SSD_CONV - 1):SLAB_TOK0 + ls],
            hst_s.reshape(1, bs, HG_HEADS, HG_K, HG_V))
```

```python
import functools

import jax
import jax.numpy as jnp
from jax import lax
from jax.experimental import pallas as pl
from jax.experimental.pallas import tpu as pltpu

F32 = jnp.float32
BF16 = jnp.bfloat16

D_MODEL = 1024
SSD_INNER = 2048
SSD_HEAD_DIM = 64
SSD_HEADS = 32
SSD_GROUPS = 4
SSD_HPG = SSD_HEADS // SSD_GROUPS
SSD_STATE = 128
SSD_CONV = 4
SSD_CONV_DIM = SSD_INNER + 2 * SSD_GROUPS * SSD_STATE
SSD_GW = SSD_HPG * SSD_HEAD_DIM
HG_HEADS = 8
HG_K = 128
HG_V = 128
HG_DIM = HG_HEADS * HG_K
MEM_LEN = 256
XA_HEADS = 4
XA_HEAD_DIM = 256
FFN_DIM = 2816
EPS = 1e-6

LANE = 128
SUBLANE = 8
VMEM_LIMIT = 56 * 1024 * 1024

CHUNK = 128
SLAB = SUBLANE
SLAB_TOK0 = 3
SLAB_NTOK = 4
NEG = -1e30
HEAD_REPS = 3

SSD_IN_W = SSD_INNER + SSD_CONV_DIM + LANE
HG_IN_W = 6 * D_MODEL


def _dot(a, b):
    return jnp.dot(a.astype(BF16), b.astype(BF16), preferred_element_type=F32)


def _dot_nt(a, b):
    return lax.dot_general(a.astype(BF16), b.astype(BF16), (((1,), (1,)), ((), ())),
                           preferred_element_type=F32)


def _dot_tn(a, b):
    return lax.dot_general(a.astype(BF16), b.astype(BF16), (((0,), (0,)), ((), ())),
                           preferred_element_type=F32)


def _split2(x):
    hi = x.astype(BF16)
    lo = (x - hi.astype(F32)).astype(BF16)
    return hi, lo


def _split3(x):
    hi = x.astype(BF16)
    r = x - hi.astype(F32)
    mid = r.astype(BF16)
    lo = (r - mid.astype(F32)).astype(BF16)
    return hi, mid, lo


def _sel_dot_l(sel, x):
    hi, mid, lo = _split3(x)
    d = lambda p: jnp.dot(sel, p, preferred_element_type=F32)
    return d(hi) + d(mid) + d(lo)


def _sel_dot_r(x, sel):
    hi, mid, lo = _split3(x)
    d = lambda p: jnp.dot(p, sel, preferred_element_type=F32)
    return d(hi) + d(mid) + d(lo)


def _sigmoid(x):
    return 1.0 / (1.0 + jnp.exp(-x))


def _silu(x):
    return x * _sigmoid(x)


def _softplus(x):
    return jnp.maximum(x, 0.0) + jnp.log(1.0 + jnp.exp(-jnp.abs(x)))


def _rms(x, w):
    return x * lax.rsqrt(jnp.mean(x * x, axis=-1, keepdims=True) + EPS) * w


def _group_rms(x, w, groups):
    gw = x.shape[-1] // groups
    parts = []
    for g in range(groups):
        xg = x[:, g * gw:(g + 1) * gw]
        parts.append(xg * lax.rsqrt(jnp.mean(xg * xg, axis=-1, keepdims=True) + EPS))
    return jnp.concatenate(parts, axis=-1) * w


def _iota(shape, dim):
    return lax.broadcasted_iota(jnp.int32, shape, dim)


def _blk_row(x, blk, row):
    r, w = x.shape
    if blk >= SUBLANE:
        x3 = x.reshape(r // blk, blk, w)
        return jnp.broadcast_to(x3[:, row:row + 1, :], x3.shape).reshape(r, w)
    x3 = x.reshape(r // SUBLANE, SUBLANE, w)
    sub = _iota(x3.shape, 1)
    out = jnp.broadcast_to(x3[:, row:row + 1, :], x3.shape)
    for b in range(1, SUBLANE // blk):
        src = jnp.broadcast_to(x3[:, b * blk + row:b * blk + row + 1, :], x3.shape)
        out = jnp.where(sub >= b * blk, src, out)
    return out.reshape(r, w)


def _blk_id(i, blk):
    return lax.shift_right_logical(i, blk.bit_length() - 1)


def _causal_mask(n, blk):
    t = _iota((n, n), 0)
    s = _iota((n, n), 1)
    return (s <= t) & (_blk_id(t, blk) == _blk_id(s, blk))


def _cumsum_matrix(n, blk):
    return _causal_mask(n, blk).astype(BF16)


def _head_expand_matrix():
    r = _iota((LANE, SSD_INNER), 0)
    c = _iota((LANE, SSD_INNER), 1)
    return ((_blk_id(c, SSD_HEAD_DIM) == (r & (SSD_HEADS - 1))) & (r < HEAD_REPS * SSD_HEADS)).astype(BF16)


def _expand_heads(x, expand):
    hi = x.astype(BF16).astype(F32)
    rest = x - hi
    mid = rest.astype(BF16).astype(F32)
    lane = _iota(x.shape, 1)
    parts = jnp.where(lane < SSD_HEADS, hi, jnp.where(lane < 2 * SSD_HEADS, mid, rest - mid))
    return jnp.dot(parts.astype(BF16), expand, preferred_element_type=F32)


def _ssd_terms(dt_raw, dt_bias, a_log, csum, last_of, valid=None):
    dt = _softplus(dt_raw + dt_bias)
    if valid is not None:
        dt = jnp.where(valid, dt, 0.0)
    da = dt * (-jnp.exp(a_log))
    cum = _sel_dot_l(csum, da)
    cum_last = last_of(cum)
    return dt, cum, jnp.exp(cum), jnp.exp(cum_last - cum) * dt, jnp.exp(cum_last)


def _ssd_intra(cum, cb_of_group, xdt, mask):
    n = cum.shape[0]
    cum_t = cum.T
    lane = _iota((n, LANE), 1)
    outs = []
    for g in range(SSD_GROUPS):
        cb = cb_of_group(g)
        for pair in range(SSD_HPG // 2):
            ms = []
            for j in range(2):
                h = g * SSD_HPG + 2 * pair + j
                diff = cum[:, h:h + 1] - cum_t[h:h + 1, :]
                ms.append((cb * jnp.exp(jnp.where(mask, diff, NEG))).astype(BF16))
            col = (g * SSD_HPG + 2 * pair) * SSD_HEAD_DIM
            xp = xdt[:, col:col + LANE]
            rhs = jnp.concatenate([jnp.where(lane < SSD_HEAD_DIM, xp, 0.0),
                                   jnp.where(lane >= SSD_HEAD_DIM, xp, 0.0)], axis=0).astype(BF16)
            outs.append(jnp.dot(jnp.concatenate(ms, axis=1), rhs, preferred_element_type=F32))
    return jnp.concatenate(outs, axis=1)


def _conv_silu(win_ref, n, conv_w, conv_b):
    acc = conv_b
    for d in range(SSD_CONV):
        acc = acc + win_ref[pl.ds(SUBLANE - d, n), :] * conv_w[SSD_CONV - 1 - d:SSD_CONV - d, :]
    return _silu(acc)


def _hgrn_gates(hq, hf, lb):
    e = jnp.exp(-jnp.abs(hf))
    r = 1.0 / (1.0 + e)
    er = e * r
    pos = hf >= 0.0
    sig = jnp.where(pos, r, er)
    nsig = jnp.where(pos, er, r)
    return _silu(hq), (1.0 - lb) * nsig, jnp.log(lb + (1.0 - lb) * sig)


def _lower_bound(lb_raw):
    m = jnp.max(lb_raw, axis=0, keepdims=True)
    e = jnp.exp(lb_raw - m)
    return e[0:1, :] / jnp.sum(e, axis=0, keepdims=True)


def _hgrn_scores(q, k, g, chunk):
    n = q.shape[0]
    row = _iota(q.shape, 0)
    ti = _iota((n, n), 0)
    si = _iota((n, n), 1)
    zs, masks = [], []
    c = 1
    while c < chunk:
        c2 = 2 * c
        second = (row & (c2 - 1)) >= c
        dist = jnp.abs(g - _blk_row(g, c2, c - 1))
        zs.append((jnp.where(second, q, k) * jnp.exp(-dist)).astype(BF16))
        masks.append((_blk_id(ti, c2) == _blk_id(si, c2)) & ((ti & (c2 - 1)) >= c) & ((si & (c2 - 1)) < c))
        c = c2
    qb, kb = q.astype(BF16), k.astype(BF16)
    outs = []
    for h in range(q.shape[1] // HG_K):
        sl = slice(h * HG_K, (h + 1) * HG_K)
        a = jnp.where(ti == si, _dot_nt(qb[:, sl], kb[:, sl]), 0.0)
        for z, m in zip(zs, masks):
            a = jnp.where(m, _dot_nt(z[:, sl], z[:, sl]), a)
        outs.append(a)
    return outs


def _rms_matmul_kernel(x_ref, nw_ref, w_ref, *o_refs):
    res = _dot(_rms(x_ref[...], nw_ref[...]), w_ref[...])
    for o_ref in o_refs:
        o_ref[...] = res.astype(o_ref.dtype)


def _rms_matmul(x, norm_w, w, tm, name, out_dtypes=(F32,)):
    m, k = x.shape
    n = w.shape[1]
    return pl.pallas_call(
        _rms_matmul_kernel,
        out_shape=tuple(jax.ShapeDtypeStruct((m, n), d) for d in out_dtypes),
        grid=(m // tm,),
        in_specs=[pl.BlockSpec((tm, k), lambda i: (i, 0)),
                  pl.BlockSpec((1, k), lambda i: (0, 0)),
                  pl.BlockSpec((k, n), lambda i: (0, 0))],
        out_specs=tuple(pl.BlockSpec((tm, n), lambda i: (i, 0)) for _ in out_dtypes),
        compiler_params=pltpu.CompilerParams(dimension_semantics=("arbitrary",),
                                             vmem_limit_bytes=VMEM_LIMIT),
        name=name,
    )(x, norm_w, w)


def _ssd_prompt_kernel(x_ref, nw_ref, win_ref, convw_ref, convb_ref, dtb_ref, alog_ref, dskip_ref,
                       snorm_ref, wso_ref, out_ref, ssm_ref, conv_ref, ht_ref, xwin_ref):
    c = pl.program_id(1)
    n = CHUNK

    @pl.when(c == 0)
    def _():
        ht_ref[...] = jnp.zeros_like(ht_ref)
        xwin_ref[0:SUBLANE, :] = jnp.zeros((SUBLANE, SSD_CONV_DIM), F32)

    xn = _rms(x_ref[0], nw_ref[...]).astype(BF16)
    z = jnp.dot(xn, win_ref[:, 0:SSD_INNER], preferred_element_type=F32)
    xbc = jnp.dot(xn, win_ref[:, SSD_INNER:SSD_INNER + SSD_CONV_DIM], preferred_element_type=F32)
    dt_raw = jnp.dot(xn, win_ref[:, SSD_INNER + SSD_CONV_DIM:SSD_IN_W], preferred_element_type=F32)

    xwin_ref[SUBLANE:SUBLANE + n, :] = xbc
    xa = _conv_silu(xwin_ref, n, convw_ref[...], convb_ref[...])
    xwin_ref[0:SUBLANE, :] = xbc[n - SUBLANE:n, :]
    xs = xa[:, 0:SSD_INNER]
    bm = xa[:, SSD_INNER:SSD_INNER + SSD_GROUPS * SSD_STATE]
    cm = xa[:, SSD_INNER + SSD_GROUPS * SSD_STATE:SSD_CONV_DIM]

    csum = _cumsum_matrix(n, n)
    dt, cum, ecum, wend, dec = _ssd_terms(dt_raw, dtb_ref[...], alog_ref[...], csum,
                                          lambda a: a[n - 1:n, :])
    wide = _expand_heads(jnp.concatenate([dt, ecum, wend, jnp.broadcast_to(dec, (SUBLANE, LANE))], axis=0),
                         _head_expand_matrix())
    dt_x, ecum_x, wend_x, dec_x = wide[0:n], wide[n:2 * n], wide[2 * n:3 * n], wide[3 * n:3 * n + 1]

    cgs = [cm[:, g * SSD_STATE:(g + 1) * SSD_STATE].astype(BF16) for g in range(SSD_GROUPS)]
    bgs = [bm[:, g * SSD_STATE:(g + 1) * SSD_STATE].astype(BF16) for g in range(SSD_GROUPS)]
    y = _ssd_intra(cum, lambda g: _dot_nt(cgs[g], bgs[g]), xs * dt_x, _causal_mask(n, n))

    xw = xs * wend_x
    inter = []
    for g in range(SSD_GROUPS):
        sl = slice(g * SSD_GW, (g + 1) * SSD_GW)
        ht = ht_ref[g]
        inter.append(_dot(cgs[g], ht))
        ht_ref[g] = ht * dec_x[:, sl] + _dot_tn(bgs[g], xw[:, sl])
    y = y + jnp.concatenate(inter, axis=1) * ecum_x + xs * dskip_ref[...]
    y = y * _silu(z)
    out_ref[0] = _dot(_group_rms(y, snorm_ref[...], SSD_GROUPS), wso_ref[...])

    @pl.when(c == pl.num_programs(1) - 1)
    def _():
        for g in range(SSD_GROUPS):
            ssm_ref[0, g * SSD_GW:(g + 1) * SSD_GW, :] = ht_ref[g].T
        conv_ref[0] = xwin_ref[0:SUBLANE, :]


def _ssd_prompt(x, norm_mix, w_ssd_in, conv_w, conv_b, dt_bias, a_log, d_skip_x, ssd_norm, w_ssd_out):
    b, l, _ = x.shape
    const = lambda shape: pl.BlockSpec(shape, lambda i, j: (0,) * len(shape))
    return pl.pallas_call(
        _ssd_prompt_kernel,
        out_shape=(jax.ShapeDtypeStruct((b, l, D_MODEL), F32),
                   jax.ShapeDtypeStruct((b, SSD_INNER, SSD_STATE), F32),
                   jax.ShapeDtypeStruct((b, SUBLANE, SSD_CONV_DIM), F32)),
        grid=(b, l // CHUNK),
        in_specs=[pl.BlockSpec((1, CHUNK, D_MODEL), lambda i, j: (i, j, 0)),
                  const((1, D_MODEL)), const((D_MODEL, SSD_IN_W)),
                  const((SSD_CONV, SSD_CONV_DIM)), const((1, SSD_CONV_DIM)),
                  const((1, LANE)), const((1, LANE)), const((1, SSD_INNER)), const((1, SSD_INNER)),
                  const((SSD_INNER, D_MODEL))],
        out_specs=(pl.BlockSpec((1, CHUNK, D_MODEL), lambda i, j: (i, j, 0)),
                   pl.BlockSpec((1, SSD_INNER, SSD_STATE), lambda i, j: (i, 0, 0)),
                   pl.BlockSpec((1, SUBLANE, SSD_CONV_DIM), lambda i, j: (i, 0, 0))),
        scratch_shapes=[pltpu.VMEM((SSD_GROUPS, SSD_STATE, SSD_GW), F32),
                        pltpu.VMEM((SUBLANE + CHUNK, SSD_CONV_DIM), F32)],
        compiler_params=pltpu.CompilerParams(dimension_semantics=("arbitrary", "arbitrary"),
                                             vmem_limit_bytes=VMEM_LIMIT),
        name="ssd_prompt",
    )(x, norm_mix, w_ssd_in, conv_w, conv_b, dt_bias, a_log, d_skip_x, ssd_norm, w_ssd_out)


def _merge_out(x, ssd_out, hg_pre, g_ssd, g_hg, whg_ref, wout_ref):
    hg_out = _dot(hg_pre, whg_ref[...])
    merged = _sigmoid(g_ssd) * ssd_out + _sigmoid(g_hg) * hg_out
    return x + _dot(merged, wout_ref[...])


def _hgrn_prompt_kernel(x_ref, ssd_ref, nw_ref, win_ref, lb_ref, hnorm_ref, whg_ref, wout_ref,
                        out_ref, hst_ref, st_ref):
    c = pl.program_id(1)
    n = CHUNK

    @pl.when(c == 0)
    def _():
        st_ref[...] = jnp.zeros_like(st_ref)

    x = x_ref[0]
    xn = _rms(x, nw_ref[...]).astype(BF16)
    seg = lambda i: jnp.dot(xn, win_ref[:, i * D_MODEL:(i + 1) * D_MODEL], preferred_element_type=F32)
    lb = _lower_bound(lb_ref[...])
    q, k, logf = _hgrn_gates(seg(0), seg(1), lb)
    v = seg(2)
    g = _sel_dot_l(_cumsum_matrix(n, n), logf)
    g_last = g[n - 1:n, :]
    scores = _hgrn_scores(q, k, g, n)
    q_in = (q * jnp.exp(g)).astype(BF16)
    k_out = (k * jnp.exp(g_last - g)).astype(BF16)
    s_dec = jnp.exp(g_last)
    vb = v.astype(BF16)
    outs = []
    for h in range(HG_HEADS):
        sl = slice(h * HG_K, (h + 1) * HG_K)
        st = st_ref[h]
        outs.append(_dot(scores[h], vb[:, sl]) + _dot_nt(q_in[:, sl], st))
        st_ref[h] = st * s_dec[:, sl] + _dot_tn(vb[:, sl], k_out[:, sl])
    o = _group_rms(jnp.concatenate(outs, axis=1), hnorm_ref[...], HG_HEADS) * _silu(seg(3))
    out_ref[0] = _merge_out(x, ssd_ref[0], o, seg(4), seg(5), whg_ref, wout_ref)

    @pl.when(c == pl.num_programs(1) - 1)
    def _():
        for h in range(HG_HEADS):
            hst_ref[0, h * HG_K:(h + 1) * HG_K, :] = st_ref[h].T


def _hgrn_prompt(x, ssd_out, norm_mix, w_hg_in, hgrn_lb, hgrn_norm, w_hgrn_out, w_out):
    b, l, _ = x.shape
    const = lambda shape: pl.BlockSpec(shape, lambda i, j: (0,) * len(shape))
    tok = pl.BlockSpec((1, CHUNK, D_MODEL), lambda i, j: (i, j, 0))
    return pl.pallas_call(
        _hgrn_prompt_kernel,
        out_shape=(jax.ShapeDtypeStruct((b, l, D_MODEL), F32),
                   jax.ShapeDtypeStruct((b, HG_DIM, HG_V), F32)),
        grid=(b, l // CHUNK),
        in_specs=[tok, tok, const((1, D_MODEL)), const((D_MODEL, HG_IN_W)), const((2, HG_DIM)),
                  const((1, HG_DIM)), const((HG_DIM, D_MODEL)), const((D_MODEL, D_MODEL))],
        out_specs=(tok, pl.BlockSpec((1, HG_DIM, HG_V), lambda i, j: (i, 0, 0))),
        scratch_shapes=[pltpu.VMEM((HG_HEADS, HG_V, HG_K), F32)],
        compiler_params=pltpu.CompilerParams(dimension_semantics=("arbitrary", "arbitrary"),
                                             vmem_limit_bytes=VMEM_LIMIT),
        name="hgrn_prompt",
    )(x, ssd_out, norm_mix, w_hg_in, hgrn_lb, hgrn_norm, w_hgrn_out, w_out)


def _attend_head(q, keys, values):
    s = _dot_nt(q, keys) * (XA_HEAD_DIM ** -0.5)
    p = jnp.exp(s - jnp.max(s, axis=-1, keepdims=True))
    p = p / jnp.sum(p, axis=-1, keepdims=True)
    return _dot(p, values)


def _attn_sample_kernel(q_ref, k_ref, v_ref, o_ref):
    nrow = MEM_LEN * XA_HEADS
    qrows = XA_HEADS * SLAB
    own = (_iota((qrows, nrow), 1) & (XA_HEADS - 1)) == _blk_id(_iota((qrows, nrow), 0), SLAB)
    for i in range(k_ref.shape[1]):
        rows = slice(i * SLAB, (i + 1) * SLAB)
        q = jnp.concatenate([q_ref[rows, h * XA_HEAD_DIM:(h + 1) * XA_HEAD_DIM] for h in range(XA_HEADS)], axis=0)
        s = _dot_nt(q, k_ref[0, i].reshape(nrow, XA_HEAD_DIM)) * (XA_HEAD_DIM ** -0.5)
        s = jnp.where(own, s, NEG)
        p = jnp.exp(s - jnp.max(s, axis=-1, keepdims=True))
        p = p / jnp.sum(p, axis=-1, keepdims=True)
        o = _dot(p, v_ref[0, i].reshape(nrow, XA_HEAD_DIM))
        for h in range(XA_HEADS):
            o_ref[rows, h * XA_HEAD_DIM:(h + 1) * XA_HEAD_DIM] = o[h * SLAB:(h + 1) * SLAB, :]


def _attn_sample(q, cache_k, cache_v, nb):
    b = cache_k.shape[1]
    kvspec = pl.BlockSpec((1, nb, MEM_LEN, XA_HEADS, XA_HEAD_DIM), lambda i: (0, i, 0, 0, 0))
    tok = pl.BlockSpec((nb * SLAB, D_MODEL), lambda i: (i, 0))
    return pl.pallas_call(
        _attn_sample_kernel,
        out_shape=jax.ShapeDtypeStruct((b * SLAB, D_MODEL), F32),
        grid=(b // nb,),
        in_specs=[tok, kvspec, kvspec],
        out_specs=tok,
        compiler_params=pltpu.CompilerParams(dimension_semantics=("arbitrary",),
                                             vmem_limit_bytes=VMEM_LIMIT),
        name="attn_sample",
    )(q, cache_k, cache_v)


def _co_ffn_tail(h, attn, wco_ref, nffn_ref, wg_ref, wu_ref, wd_ref, nfin_ref):
    h = h + _dot(attn, wco_ref[...])
    hn = _rms(h, nffn_ref[...]).astype(BF16)
    gate = jnp.dot(hn, wg_ref[...], preferred_element_type=F32)
    up = jnp.dot(hn, wu_ref[...], preferred_element_type=F32)
    h = h + _dot(_silu(gate) * up, wd_ref[...])
    return _rms(h, nfin_ref[...])


def _co_ffn_kernel(h_ref, a_ref, *refs):
    refs[-1][...] = _co_ffn_tail(h_ref[...], a_ref[...], *refs[:-1])


def _attn_ffn_kernel(h_ref, ncross_ref, wq_ref, kv_ref, *refs):
    h = h_ref[0]
    q = _dot(_rms(h, ncross_ref[...]), wq_ref[...])
    heads = []
    for i in range(XA_HEADS):
        sl = slice(i * XA_HEAD_DIM, (i + 1) * XA_HEAD_DIM)
        heads.append(_attend_head(q[:, sl], kv_ref[0, :, sl], kv_ref[0, :, D_MODEL + sl.start:D_MODEL + sl.stop]))
    refs[-1][0] = _co_ffn_tail(h, jnp.concatenate(heads, axis=1), *refs[:-1])


def _ffn_weight_specs():
    const = lambda shape: pl.BlockSpec(shape, lambda *_: (0, 0), pipeline_mode=pl.Buffered(1))
    return [const((D_MODEL, D_MODEL)), const((1, D_MODEL)), const((D_MODEL, FFN_DIM)), const((D_MODEL, FFN_DIM)),
            const((FFN_DIM, D_MODEL)), const((1, D_MODEL))]


def _co_ffn(h1, attn, ffn_weights, tm, name):
    m = h1.shape[0]
    tok = pl.BlockSpec((tm, D_MODEL), lambda i: (i, 0))
    return pl.pallas_call(
        _co_ffn_kernel,
        out_shape=jax.ShapeDtypeStruct((m, D_MODEL), F32),
        grid=(m // tm,),
        in_specs=[tok, tok] + _ffn_weight_specs(),
        out_specs=tok,
        compiler_params=pltpu.CompilerParams(dimension_semantics=("arbitrary",),
                                             vmem_limit_bytes=VMEM_LIMIT),
        name=name,
    )(h1, attn, *ffn_weights)


def _attn_ffn(h1, norm_cross, w_cq, kv, ffn_weights, tq, name):
    b, l, _ = h1.shape
    tok = pl.BlockSpec((1, tq, D_MODEL), lambda i, j: (i, j, 0))
    const = lambda shape: pl.BlockSpec(shape, lambda i, j: (0, 0), pipeline_mode=pl.Buffered(1))
    return pl.pallas_call(
        _attn_ffn_kernel,
        out_shape=jax.ShapeDtypeStruct((b, l, D_MODEL), F32),
        grid=(b, l // tq),
        in_specs=[tok, const((1, D_MODEL)), const((D_MODEL, D_MODEL)),
                  pl.BlockSpec((1, MEM_LEN, 2 * D_MODEL), lambda i, j: (i, 0, 0))] + _ffn_weight_specs(),
        out_specs=tok,
        compiler_params=pltpu.CompilerParams(dimension_semantics=("arbitrary", "arbitrary"),
                                             vmem_limit_bytes=VMEM_LIMIT),
        name=name,
    )(h1, norm_cross, w_cq, kv, *ffn_weights)


def _sample_pre_kernel(ps_ref, ph_ref, cst_ref, convw_ref, convb_ref, dtb_ref, alog_ref, dskip_ref, lb_ref,
                       ycore_ref, ecum_ref, dec_ref, cm_ref, bm_ref, xw_ref, conv_ref,
                       odiag_ref, qg_ref, kg_ref, v_ref, sdec_ref, xwin_ref):
    n = ps_ref.shape[0]
    row = _iota((n, LANE), 0) & (SLAB - 1)
    valid = (row >= SLAB_TOK0) & (row < SLAB_TOK0 + SLAB_NTOK)

    full = cst_ref[...] + ps_ref[:, SSD_INNER:SSD_INNER + SSD_CONV_DIM]
    conv_ref[...] = full
    xwin_ref[0:SUBLANE, :] = jnp.zeros((SUBLANE, SSD_CONV_DIM), F32)
    xwin_ref[SUBLANE:SUBLANE + n, :] = full
    xa = _conv_silu(xwin_ref, n, convw_ref[...], convb_ref[...])
    xs = xa[:, 0:SSD_INNER]
    bm = xa[:, SSD_INNER:SSD_INNER + SSD_GROUPS * SSD_STATE]
    cm = xa[:, SSD_INNER + SSD_GROUPS * SSD_STATE:SSD_CONV_DIM]

    csum = _cumsum_matrix(n, SLAB)
    dt, cum, ecum, wend, dec = _ssd_terms(ps_ref[:, SSD_INNER + SSD_CONV_DIM:SSD_IN_W], dtb_ref[...],
                                          alog_ref[...], csum, lambda a: _blk_row(a, SLAB, SLAB - 1), valid)
    wide = _expand_heads(jnp.concatenate([dt, wend, dec], axis=0), _head_expand_matrix())
    cgs = [cm[:, g * SSD_STATE:(g + 1) * SSD_STATE] for g in range(SSD_GROUPS)]
    bgs = [bm[:, g * SSD_STATE:(g + 1) * SSD_STATE] for g in range(SSD_GROUPS)]
    y = _ssd_intra(cum, lambda g: _dot_nt(cgs[g], bgs[g]), xs * wide[0:n], _causal_mask(n, SLAB))
    ycore_ref[...] = y + xs * dskip_ref[...]
    ecum_ref[...] = ecum
    dec_ref[...] = wide[2 * n:3 * n]
    cm_ref[...] = cm
    bm_ref[...] = bm
    xw_ref[...] = xs * wide[n:2 * n]

    lb = _lower_bound(lb_ref[...])
    q, k, logf = _hgrn_gates(ph_ref[:, 0:D_MODEL], ph_ref[:, D_MODEL:2 * D_MODEL], lb)
    v = ph_ref[:, 2 * D_MODEL:3 * D_MODEL]
    roww = _iota((n, HG_DIM), 0) & (SLAB - 1)
    validw = (roww >= SLAB_TOK0) & (roww < SLAB_TOK0 + SLAB_NTOK)
    k = jnp.where(validw, k, 0.0)
    logf = jnp.where(validw, logf, 0.0)
    g = _sel_dot_l(csum, logf)
    g_last = _blk_row(g, SLAB, SLAB - 1)
    scores = _hgrn_scores(q, k, g, SLAB)
    for h in range(HG_HEADS):
        sl = slice(h * HG_K, (h + 1) * HG_K)
        odiag_ref[:, sl] = _dot(scores[h], v[:, sl])
    qg_ref[...] = q * jnp.exp(g)
    kg_ref[...] = k * jnp.exp(g_last - g)
    v_ref[...] = v
    sdec_ref[...] = jnp.exp(g_last)


def _sample_pre(proj_ssd, proj_hg, conv_state, conv_w, conv_b, dt_bias, a_log, d_skip_x, hgrn_lb, tm):
    m = proj_ssd.shape[0]
    rows = lambda w: pl.BlockSpec((tm, w), lambda i: (i, 0))
    const = lambda shape: pl.BlockSpec(shape, lambda i: (0, 0))
    widths = (SSD_INNER, LANE, SSD_INNER, SSD_GROUPS * SSD_STATE, SSD_GROUPS * SSD_STATE, SSD_INNER,
              SSD_CONV_DIM, HG_DIM, HG_DIM, HG_DIM, HG_DIM, HG_DIM)
    return pl.pallas_call(
        _sample_pre_kernel,
        out_shape=tuple(jax.ShapeDtypeStruct((m, w), F32) for w in widths),
        grid=(m // tm,),
        in_specs=[rows(SSD_IN_W), rows(HG_IN_W), rows(SSD_CONV_DIM),
                  const((SSD_CONV, SSD_CONV_DIM)), const((1, SSD_CONV_DIM)), const((1, LANE)), const((1, LANE)),
                  const((1, SSD_INNER)), const((2, HG_DIM))],
        out_specs=tuple(rows(w) for w in widths),
        scratch_shapes=[pltpu.VMEM((SUBLANE + tm, SSD_CONV_DIM), F32)],
        compiler_params=pltpu.CompilerParams(dimension_semantics=("arbitrary",),
                                             vmem_limit_bytes=VMEM_LIMIT),
        name="sample_pre",
    )(proj_ssd, proj_hg, conv_state, conv_w, conv_b, dt_bias, a_log, d_skip_x, hgrn_lb)


def _decay_rows(dec):
    hi, mid, lo = (p.astype(F32) for p in _split3(dec[0:1, :]))
    r = _iota((SLAB, dec.shape[1]), 0)
    return jnp.where(r == 0, hi, jnp.where(r == 1, mid, jnp.where(r == 2, lo, 0.0))).astype(BF16)


def _state_step(state, upd_rows, upd_cols, dec):
    qn = upd_cols.shape[1]
    lhs = jnp.concatenate([upd_rows.astype(BF16), _decay_rows(dec)], axis=0)
    ones3 = (_iota((SLAB, qn), 0) < 3).astype(BF16)
    zeros = jnp.zeros((SLAB, qn), BF16)
    rhs = jnp.concatenate([jnp.concatenate([upd_cols.astype(BF16), zeros], axis=1),
                           jnp.concatenate([zeros, ones3], axis=1)], axis=0)
    both = lax.dot_general(lhs, rhs, (((0,), (0,)), ((), ())), preferred_element_type=F32)
    return state * both[:, qn:2 * qn] + both[:, 0:qn]


def _sample_state_kernel(cm_ref, bm_ref, xw_ref, dec_ref, qg_ref, kg_ref, v_ref, sdec_ref, ssm_ref, hst_ref,
                         yint_ref, oint_ref, ssm_out_ref, hst_out_ref):
    for i in range(ssm_ref.shape[0]):
        rows = slice(i * SLAB, (i + 1) * SLAB)
        for g in range(SSD_GROUPS):
            hsl = slice(g * SSD_GW, (g + 1) * SSD_GW)
            nsl = slice(g * SSD_STATE, (g + 1) * SSD_STATE)
            st = ssm_ref[i, hsl, :]
            yint_ref[rows, hsl] = _dot_nt(cm_ref[rows, nsl], st)
            ssm_out_ref[i, hsl, :] = _state_step(st, xw_ref[rows, hsl], bm_ref[rows, nsl], dec_ref[rows, hsl])
        for h in range(HG_HEADS):
            sl = slice(h * HG_K, (h + 1) * HG_K)
            st = hst_ref[i, sl, :]
            oint_ref[rows, sl] = _dot(qg_ref[rows, sl], st)
            hst_out_ref[i, sl, :] = _state_step(st, kg_ref[rows, sl], v_ref[rows, sl], sdec_ref[rows, sl])


def _sample_state(cm, bm, xw, dec, qg, kg, v, sdec, ssm, hst, nb):
    b = ssm.shape[0]
    rows = lambda w: pl.BlockSpec((nb * SLAB, w), lambda i: (i, 0))
    ssm_spec = pl.BlockSpec((nb, SSD_INNER, SSD_STATE), lambda i: (i, 0, 0))
    hst_spec = pl.BlockSpec((nb, HG_DIM, HG_V), lambda i: (i, 0, 0))
    return pl.pallas_call(
        _sample_state_kernel,
        out_shape=(jax.ShapeDtypeStruct((b * SLAB, SSD_INNER), F32),
                   jax.ShapeDtypeStruct((b * SLAB, HG_DIM), F32),
                   jax.ShapeDtypeStruct(ssm.shape, F32),
                   jax.ShapeDtypeStruct(hst.shape, F32)),
        grid=(b // nb,),
        in_specs=[rows(SSD_GROUPS * SSD_STATE), rows(SSD_GROUPS * SSD_STATE), rows(SSD_INNER), rows(SSD_INNER),
                  rows(HG_DIM), rows(HG_DIM), rows(HG_DIM), rows(HG_DIM), ssm_spec, hst_spec],
        out_specs=(rows(SSD_INNER), rows(HG_DIM), ssm_spec, hst_spec),
        compiler_params=pltpu.CompilerParams(dimension_semantics=("arbitrary",),
                                             vmem_limit_bytes=VMEM_LIMIT),
        name="sample_state",
    )(cm, bm, xw, dec, qg, kg, v, sdec, ssm, hst)


def _sample_merge_kernel(x_ref, z_ref, gates_ref, ycore_ref, yint_ref, ecum_ref, odiag_ref, oint_ref,
                         snorm_ref, wso_ref, hnorm_ref, whg_ref, wout_ref, ncross_ref, wq_ref,
                         h_ref, q_ref):
    y = ycore_ref[...] + yint_ref[...] * _expand_heads(ecum_ref[...], _head_expand_matrix())
    y = y * _silu(z_ref[...])
    ssd_out = _dot(_group_rms(y, snorm_ref[...], SSD_GROUPS), wso_ref[...])
    o = _group_rms(odiag_ref[...] + oint_ref[...], hnorm_ref[...], HG_HEADS) * _silu(gates_ref[:, 0:D_MODEL])
    h = _merge_out(x_ref[...], ssd_out, o, gates_ref[:, D_MODEL:2 * D_MODEL], gates_ref[:, 2 * D_MODEL:3 * D_MODEL],
                   whg_ref, wout_ref)
    h_ref[...] = h
    q_ref[...] = _dot(_rms(h, ncross_ref[...]), wq_ref[...])


def _sample_merge(x, proj_ssd, proj_hg, ycore, yint, ecum, odiag, oint, ssd_norm, w_ssd_out, hgrn_norm,
                  w_hgrn_out, w_out, norm_cross, w_cq, tm):
    m = x.shape[0]
    rows = lambda w: pl.BlockSpec((tm, w), lambda i: (i, 0))
    const = lambda shape: pl.BlockSpec(shape, lambda i: (0, 0))
    return pl.pallas_call(
        _sample_merge_kernel,
        out_shape=(jax.ShapeDtypeStruct((m, D_MODEL), F32), jax.ShapeDtypeStruct((m, D_MODEL), F32)),
        grid=(m // tm,),
        in_specs=[rows(D_MODEL),
                  pl.BlockSpec((tm, SSD_INNER), lambda i: (i, 0)),
                  pl.BlockSpec((tm, 3 * D_MODEL), lambda i: (i, 1)),
                  rows(SSD_INNER), rows(SSD_INNER), rows(LANE), rows(HG_DIM), rows(HG_DIM),
                  const((1, SSD_INNER)), const((SSD_INNER, D_MODEL)), const((1, HG_DIM)),
                  const((HG_DIM, D_MODEL)), const((D_MODEL, D_MODEL)), const((1, D_MODEL)),
                  const((D_MODEL, D_MODEL))],
        out_specs=(rows(D_MODEL), rows(D_MODEL)),
        compiler_params=pltpu.CompilerParams(dimension_semantics=("arbitrary",),
                                             vmem_limit_bytes=VMEM_LIMIT),
        name="sample_merge",
    )(x, proj_ssd, proj_hg, ycore, yint, ecum, odiag, oint, ssd_norm, w_ssd_out, hgrn_norm, w_hgrn_out, w_out,
      norm_cross, w_cq)


def kernel(x_prompt, x_sample, mem_prompt, state_ssm, state_conv, state_hgrn, cache_mem_k, cache_mem_v,
           norm_mix, w_in, conv_w, conv_b, dt_bias, a_log, d_skip, ssd_norm, w_ssd_out, hgrn_lb, hgrn_norm,
           w_hgrn_out, w_out, norm_cross, norm_mem, w_cq, w_ck, w_cv, w_co, norm_ffn, w_gate, w_up, w_down,
           norm_final):
    assert norm_mix.shape[0] == 1, "single-layer configuration only"
    bp, lp, _ = x_prompt.shape
    bs, ls, _ = x_sample.shape
    assert ls == SLAB_NTOK and lp % CHUNK == 0

    row = lambda p: p[0].reshape(1, -1)
    head_lanes = lambda p: jnp.pad(jnp.tile(p, (1, HEAD_REPS)), ((0, 0), (0, LANE - HEAD_REPS * SSD_HEADS)))
    w_in0 = w_in[0]
    n_ssd = SSD_INNER + SSD_CONV_DIM
    w_ssd_in = jnp.concatenate([w_in0[:, :n_ssd], head_lanes(w_in0[:, n_ssd:n_ssd + SSD_HEADS])], axis=1).astype(BF16)
    w_hg_in = w_in0[:, n_ssd + SSD_HEADS:].astype(BF16)
    dtb = head_lanes(row(dt_bias))
    alog = head_lanes(row(a_log))
    d_skip_x = jnp.repeat(d_skip[0], SSD_HEAD_DIM).reshape(1, SSD_INNER)
    bf = lambda p: p[0].astype(BF16)
    w_so, w_ho, w_o, w_q, w_c = bf(w_ssd_out), bf(w_hgrn_out), bf(w_out), bf(w_cq), bf(w_co)
    w_g, w_u, w_d = bf(w_gate), bf(w_up), bf(w_down)
    w_kv = jnp.concatenate([w_ck[0], w_cv[0]], axis=1).astype(BF16)
    nfin = norm_final.reshape(1, D_MODEL)

    ffn_weights = (w_c, row(norm_ffn), w_g, w_u, w_d, nfin)
    kv, kv_bf = _rms_matmul(mem_prompt.reshape(bp * MEM_LEN, D_MODEL), row(norm_mem), w_kv, 256, "memory_kv",
                            (F32, BF16))
    ssd_out, ssm_p, conv_p = _ssd_prompt(x_prompt, row(norm_mix), w_ssd_in, conv_w[0], row(conv_b), dtb, alog,
                                         d_skip_x, row(ssd_norm), w_so)
    h1_p, hst_p = _hgrn_prompt(x_prompt, ssd_out, row(norm_mix), w_hg_in, hgrn_lb, row(hgrn_norm), w_ho, w_o)
    y_p = _attn_ffn(h1_p, row(norm_cross), w_q, kv_bf.reshape(bp, MEM_LEN, 2 * D_MODEL), ffn_weights, min(512, lp),
                    "attn_ffn_prompt")

    ms = bs * SLAB
    xs = jnp.pad(x_sample, ((0, 0), (SLAB_TOK0, SLAB - SLAB_TOK0 - ls), (0, 0))).reshape(ms, D_MODEL)
    cst = jnp.pad(state_conv[0], ((0, 0), (0, SLAB - (SSD_CONV - 1)), (0, 0))).reshape(ms, SSD_CONV_DIM)
    (proj_ssd,) = _rms_matmul(xs, row(norm_mix), w_ssd_in, 256, "sample_in_ssd")
    (proj_hg,) = _rms_matmul(xs, row(norm_mix), w_hg_in, 256, "sample_in_hgrn")
    (ycore, ecum, dec, cm, bm, xw, conv_full, odiag, qg, kg, vv, sdec) = _sample_pre(
        proj_ssd, proj_hg, cst, conv_w[0], row(conv_b), dtb, alog, d_skip_x, hgrn_lb, 128)
    yint, oint, ssm_s, hst_s = _sample_state(
        cm, bm, xw, dec, qg, kg, vv, sdec, state_ssm[0].reshape(bs, SSD_INNER, SSD_STATE),
        state_hgrn[0].reshape(bs, HG_DIM, HG_V), 4)
    h1_s, q_s = _sample_merge(xs, proj_ssd, proj_hg, ycore, yint, ecum, odiag, oint, row(ssd_norm), w_so,
                              row(hgrn_norm), w_ho, w_o, row(norm_cross), w_q, 256)
    att_s = _attn_sample(q_s, cache_mem_k, cache_mem_v, 4)
    y_s = _co_ffn(h1_s, att_s, ffn_weights, 256, "co_ffn_sample")

    tok = slice(SLAB_TOK0, SLAB_TOK0 + ls)
    kv4 = kv.reshape(bp, MEM_LEN, 2, XA_HEADS, XA_HEAD_DIM)
    return (y_p.reshape(bp, lp, D_MODEL),
            y_s.reshape(bs, SLAB, D_MODEL)[:, tok],
            ssm_p.reshape(1, bp, SSD_HEADS, SSD_HEAD_DIM, SSD_STATE),
            conv_p[None, :, SUBLANE - (SSD_CONV - 1):],
            hst_p.reshape(1, bp, HG_HEADS, HG_K, HG_V),
            kv4[:, :, 0][None],
            kv4[:, :, 1][None],
            ssm_s.reshape(1, bs, SSD_HEADS, SSD_HEAD_DIM, SSD_STATE),
            conv_full.reshape(bs, SLAB, SSD_CONV_DIM)[None, :, SLAB_TOK0 + ls - (SSD_CONV - 1):SLAB_TOK0 + ls],
            hst_s.reshape(1, bs, HG_HEADS, HG_K, HG_V))
```

```python
import functools

import jax
import jax.numpy as jnp
from jax import lax
from jax.experimental import pallas as pl
from jax.experimental.pallas import tpu as pltpu

F32 = jnp.float32
BF16 = jnp.bfloat16

D_MODEL = 1024
SSD_INNER = 2048
SSD_HEAD_DIM = 64
SSD_HEADS = 32
SSD_GROUPS = 4
SSD_HPG = SSD_HEADS // SSD_GROUPS
SSD_STATE = 128
SSD_CONV = 4
SSD_CONV_DIM = SSD_INNER + 2 * SSD_GROUPS * SSD_STATE
SSD_GW = SSD_HPG * SSD_HEAD_DIM
HG_HEADS = 8
HG_K = 128
HG_V = 128
HG_DIM = HG_HEADS * HG_K
MEM_LEN = 256
XA_HEADS = 4
XA_HEAD_DIM = 256
FFN_DIM = 2816
EPS = 1e-6

LANE = 128
SUBLANE = 8
VMEM_LIMIT = 56 * 1024 * 1024

CHUNK = 128
SLAB = SUBLANE
SLAB_TOK0 = 3
SLAB_NTOK = 4
NEG = -1e30
HEAD_REPS = 3

SSD_IN_W = SSD_INNER + SSD_CONV_DIM + LANE
HG_IN_W = 6 * D_MODEL


def _dot(a, b):
    return jnp.dot(a.astype(BF16), b.astype(BF16), preferred_element_type=F32)


def _dot_nt(a, b):
    return lax.dot_general(a.astype(BF16), b.astype(BF16), (((1,), (1,)), ((), ())),
                           preferred_element_type=F32)


def _dot_tn(a, b):
    return lax.dot_general(a.astype(BF16), b.astype(BF16), (((0,), (0,)), ((), ())),
                           preferred_element_type=F32)


def _split2(x):
    hi = x.astype(BF16)
    lo = (x - hi.astype(F32)).astype(BF16)
    return hi, lo


def _split3(x):
    hi = x.astype(BF16)
    r = x - hi.astype(F32)
    mid = r.astype(BF16)
    lo = (r - mid.astype(F32)).astype(BF16)
    return hi, mid, lo


def _sel_dot_l(sel, x, parts=3):
    d = lambda p: jnp.dot(sel, p, preferred_element_type=F32)
    if parts == 2:
        hi, lo = _split2(x)
        return d(hi) + d(lo)
    hi, mid, lo = _split3(x)
    return d(hi) + d(mid) + d(lo)


def _sel_dot_r(x, sel):
    hi, mid, lo = _split3(x)
    d = lambda p: jnp.dot(p, sel, preferred_element_type=F32)
    return d(hi) + d(mid) + d(lo)


def _sigmoid(x):
    return 0.5 * jnp.tanh(0.5 * x) + 0.5


def _silu(x):
    h = 0.5 * x
    return h * jnp.tanh(h) + h


def _softplus(x):
    return jnp.maximum(x, 0.0) + jnp.log(1.0 + jnp.exp(-jnp.abs(x)))


def _rms(x, w):
    return x * lax.rsqrt(jnp.mean(x * x, axis=-1, keepdims=True) + EPS) * w


def _group_rms(x, w, groups):
    gw = x.shape[-1] // groups
    parts = []
    for g in range(groups):
        xg = x[:, g * gw:(g + 1) * gw]
        parts.append(xg * lax.rsqrt(jnp.mean(xg * xg, axis=-1, keepdims=True) + EPS))
    return jnp.concatenate(parts, axis=-1) * w


def _iota(shape, dim):
    return lax.broadcasted_iota(jnp.int32, shape, dim)


def _blk_row(x, blk, row):
    r, w = x.shape
    if blk >= SUBLANE:
        x3 = x.reshape(r // blk, blk, w)
        return jnp.broadcast_to(x3[:, row:row + 1, :], x3.shape).reshape(r, w)
    x3 = x.reshape(r // SUBLANE, SUBLANE, w)
    sub = _iota(x3.shape, 1)
    out = jnp.broadcast_to(x3[:, row:row + 1, :], x3.shape)
    for b in range(1, SUBLANE // blk):
        src = jnp.broadcast_to(x3[:, b * blk + row:b * blk + row + 1, :], x3.shape)
        out = jnp.where(sub >= b * blk, src, out)
    return out.reshape(r, w)


def _blk_id(i, blk):
    return lax.shift_right_logical(i, blk.bit_length() - 1)


def _causal_mask(n, blk):
    t = _iota((n, n), 0)
    s = _iota((n, n), 1)
    return (s <= t) & (_blk_id(t, blk) == _blk_id(s, blk))


def _cumsum_matrix(n, blk):
    return _causal_mask(n, blk).astype(BF16)


def _head_expand_matrix():
    r = _iota((LANE, SSD_INNER), 0)
    c = _iota((LANE, SSD_INNER), 1)
    return ((_blk_id(c, SSD_HEAD_DIM) == (r & (SSD_HEADS - 1))) & (r < HEAD_REPS * SSD_HEADS)).astype(BF16)


def _expand_heads(x, expand):
    hi = x.astype(BF16).astype(F32)
    rest = x - hi
    mid = rest.astype(BF16).astype(F32)
    lane = _iota(x.shape, 1)
    parts = jnp.where(lane < SSD_HEADS, hi, jnp.where(lane < 2 * SSD_HEADS, mid, rest - mid))
    return jnp.dot(parts.astype(BF16), expand, preferred_element_type=F32)


def _ssd_terms(dt_raw, dt_bias, a_log, csum, last_of, valid=None):
    dt = _softplus(dt_raw + dt_bias)
    if valid is not None:
        dt = jnp.where(valid, dt, 0.0)
    da = dt * (-jnp.exp(a_log))
    cum = _sel_dot_l(csum, da)
    cum_last = last_of(cum)
    return dt, cum, jnp.exp(cum), jnp.exp(cum_last - cum) * dt, jnp.exp(cum_last)


def _ssd_intra(cum, cb_of_group, xdt, mask):
    n = cum.shape[0]
    cum_t = cum.T
    lane = _iota((n, LANE), 1)
    outs = []
    for g in range(SSD_GROUPS):
        cb = cb_of_group(g)
        for pair in range(SSD_HPG // 2):
            ms = []
            for j in range(2):
                h = g * SSD_HPG + 2 * pair + j
                diff = cum[:, h:h + 1] - cum_t[h:h + 1, :]
                ms.append((cb * jnp.exp(jnp.where(mask, diff, NEG))).astype(BF16))
            col = (g * SSD_HPG + 2 * pair) * SSD_HEAD_DIM
            xp = xdt[:, col:col + LANE]
            rhs = jnp.concatenate([jnp.where(lane < SSD_HEAD_DIM, xp, 0.0),
                                   jnp.where(lane >= SSD_HEAD_DIM, xp, 0.0)], axis=0).astype(BF16)
            outs.append(jnp.dot(jnp.concatenate(ms, axis=1), rhs, preferred_element_type=F32))
    return jnp.concatenate(outs, axis=1)


def _conv_silu(tail, rows, conv_w, conv_b):
    n = rows.shape[0]
    ext = jnp.concatenate([tail, rows], axis=0)
    acc = conv_b + rows * conv_w[SSD_CONV - 1:SSD_CONV, :]
    for d in range(1, SSD_CONV):
        shifted = pltpu.roll(ext, d, axis=0)[SUBLANE:SUBLANE + n]
        acc = acc + shifted * conv_w[SSD_CONV - 1 - d:SSD_CONV - d, :]
    return _silu(acc)


def _hgrn_gates(hq, hf, lb):
    b = 0.5 - 0.5 * lb
    bt = b * jnp.tanh(0.5 * hf)
    return _silu(hq), b - bt, jnp.log((0.5 + 0.5 * lb) + bt)


def _lower_bound(lb_raw):
    m = jnp.max(lb_raw, axis=0, keepdims=True)
    e = jnp.exp(lb_raw - m)
    return e[0:1, :] / jnp.sum(e, axis=0, keepdims=True)


def _hgrn_scores(q, k, g, chunk):
    n = q.shape[0]
    row = _iota(q.shape, 0)
    ti = _iota((n, n), 0)
    si = _iota((n, n), 1)
    zs, masks = [], []
    c = 1
    while c < chunk:
        c2 = 2 * c
        second = (row & (c2 - 1)) >= c
        dist = jnp.abs(g - _blk_row(g, c2, c - 1))
        zs.append((jnp.where(second, q, k) * jnp.exp(-dist)).astype(BF16))
        masks.append((_blk_id(ti, c2) == _blk_id(si, c2)) & ((ti & (c2 - 1)) >= c) & ((si & (c2 - 1)) < c))
        c = c2
    terms = [(z, z, m) for z, m in zip(zs, masks)] + [(q.astype(BF16), k.astype(BF16), ti == si)]
    zero = jnp.zeros((n, HG_K), BF16)
    outs = []
    for h in range(q.shape[1] // HG_K):
        sl = slice(h * HG_K, (h + 1) * HG_K)
        a = jnp.zeros((n, n), F32)
        for i in range(0, len(terms) - 1, 2):
            (la, ra, ma), (lb, rb, mb) = terms[i], terms[i + 1]
            rhs = jnp.concatenate([jnp.concatenate([ra[:, sl], zero], axis=1),
                                   jnp.concatenate([zero, rb[:, sl]], axis=1)], axis=0)
            both = _dot_nt(jnp.concatenate([la[:, sl], lb[:, sl]], axis=1), rhs)
            a = jnp.where(ma, both[:, 0:n], jnp.where(mb, both[:, n:2 * n], a))
        if len(terms) % 2:
            la, ra, ma = terms[-1]
            a = jnp.where(ma, _dot_nt(la[:, sl], ra[:, sl]), a)
        outs.append(a)
    return outs


def _rms_matmul_kernel(x_ref, nw_ref, w_ref, *o_refs):
    res = _dot(_rms(x_ref[...], nw_ref[...]), w_ref[...])
    for o_ref in o_refs:
        o_ref[...] = res.astype(o_ref.dtype)


def _rms_matmul(x, norm_w, w, tm, name, out_dtypes=(F32,)):
    m, k = x.shape
    n = w.shape[1]
    return pl.pallas_call(
        _rms_matmul_kernel,
        out_shape=tuple(jax.ShapeDtypeStruct((m, n), d) for d in out_dtypes),
        grid=(m // tm,),
        in_specs=[pl.BlockSpec((tm, k), lambda i: (i, 0)),
                  pl.BlockSpec((1, k), lambda i: (0, 0)),
                  pl.BlockSpec((k, n), lambda i: (0, 0))],
        out_specs=tuple(pl.BlockSpec((tm, n), lambda i: (i, 0)) for _ in out_dtypes),
        compiler_params=pltpu.CompilerParams(dimension_semantics=("arbitrary",),
                                             vmem_limit_bytes=VMEM_LIMIT),
        name=name,
    )(x, norm_w, w)


def _project_into(p_ref, x, nw_ref, w_ref, seg):
    xn = _rms(x, nw_ref[...]).astype(BF16)
    for c0 in range(0, w_ref.shape[1], seg):
        c1 = min(c0 + seg, w_ref.shape[1])
        p_ref[:, c0:c1] = jnp.dot(xn, w_ref[:, c0:c1], preferred_element_type=F32)


def _skewed_chunks(xa_ref, xb_ref, p0_ref, p1_ref, project, scan, store):
    n = CHUNK

    @pl.when(pl.program_id(1) == 0)
    def _():
        project(p0_ref, xa_ref[0, 0:n, :])

    project(p1_ref, xa_ref[0, n:2 * n, :])
    store(0, scan(p0_ref, xa_ref[0, 0:n, :], 0))
    project(p0_ref, xb_ref[0])
    store(1, scan(p1_ref, xa_ref[0, n:2 * n, :], 1))


def _ssd_chunk(p_ref, tail_ref, convw_ref, convb_ref, dtb_ref, alog_ref, dskip_ref, snorm_ref, wso_ref, ht_ref):
    n = CHUNK
    xbc = p_ref[:, SSD_INNER:SSD_INNER + SSD_CONV_DIM]
    xa = _conv_silu(tail_ref[...], xbc, convw_ref[...], convb_ref[...])
    tail_ref[...] = xbc[n - SUBLANE:n, :]
    xs = xa[:, 0:SSD_INNER]
    bm = xa[:, SSD_INNER:SSD_INNER + SSD_GROUPS * SSD_STATE]
    cm = xa[:, SSD_INNER + SSD_GROUPS * SSD_STATE:SSD_CONV_DIM]

    csum = _cumsum_matrix(n, n)
    dt, cum, ecum, wend, dec = _ssd_terms(p_ref[:, SSD_INNER + SSD_CONV_DIM:SSD_IN_W], dtb_ref[...], alog_ref[...],
                                          csum, lambda a: a[n - 1:n, :])
    wide = _expand_heads(jnp.concatenate([dt, ecum, wend, jnp.broadcast_to(dec, (SUBLANE, LANE))], axis=0),
                         _head_expand_matrix())
    dt_x, ecum_x, wend_x, dec_x = wide[0:n], wide[n:2 * n], wide[2 * n:3 * n], wide[3 * n:3 * n + 1]

    cgs = [cm[:, g * SSD_STATE:(g + 1) * SSD_STATE].astype(BF16) for g in range(SSD_GROUPS)]
    bgs = [bm[:, g * SSD_STATE:(g + 1) * SSD_STATE].astype(BF16) for g in range(SSD_GROUPS)]
    y = _ssd_intra(cum, lambda g: _dot_nt(cgs[g], bgs[g]), xs * dt_x, _causal_mask(n, n))

    xw = xs * wend_x
    inter = []
    for g in range(SSD_GROUPS):
        sl = slice(g * SSD_GW, (g + 1) * SSD_GW)
        ht = ht_ref[g]
        inter.append(_dot(cgs[g], ht))
        ht_ref[g] = ht * dec_x[:, sl] + _dot_tn(bgs[g], xw[:, sl])
    y = y + jnp.concatenate(inter, axis=1) * ecum_x + xs * dskip_ref[...]
    y = y * _silu(p_ref[:, 0:SSD_INNER])
    return _dot(_group_rms(y, snorm_ref[...], SSD_GROUPS), wso_ref[...])


def _ssd_prompt_kernel(xa_ref, xb_ref, nw_ref, win_ref, convw_ref, convb_ref, dtb_ref, alog_ref, dskip_ref,
                       snorm_ref, wso_ref, out_ref, ssm_ref, conv_ref, ht_ref, tail_ref, p0_ref, p1_ref):
    @pl.when(pl.program_id(1) == 0)
    def _():
        ht_ref[...] = jnp.zeros_like(ht_ref)
        tail_ref[...] = jnp.zeros_like(tail_ref)

    def store(half, val):
        out_ref[0, half * CHUNK:(half + 1) * CHUNK, :] = val

    _skewed_chunks(
        xa_ref, xb_ref, p0_ref, p1_ref,
        lambda p_ref, x: _project_into(p_ref, x, nw_ref, win_ref, D_MODEL),
        lambda p_ref, x, half: _ssd_chunk(p_ref, tail_ref, convw_ref, convb_ref, dtb_ref, alog_ref, dskip_ref,
                                          snorm_ref, wso_ref, ht_ref),
        store)

    @pl.when(pl.program_id(1) == pl.num_programs(1) - 1)
    def _():
        for g in range(SSD_GROUPS):
            ssm_ref[0, g * SSD_GW:(g + 1) * SSD_GW, :] = ht_ref[g].T
        conv_ref[0] = tail_ref[...]


def _chunk_pair_specs(nchunks):
    pair = pl.BlockSpec((1, 2 * CHUNK, D_MODEL), lambda i, j: (i, j, 0))
    ahead = pl.BlockSpec((1, CHUNK, D_MODEL), lambda i, j: (i, jnp.minimum(2 * j + 2, nchunks - 1), 0))
    return pair, ahead


def _ssd_prompt(x, norm_mix, w_ssd_in, conv_w, conv_b, dt_bias, a_log, d_skip_x, ssd_norm, w_ssd_out):
    b, l, _ = x.shape
    const = lambda shape: pl.BlockSpec(shape, lambda i, j: (0, 0), pipeline_mode=pl.Buffered(1))
    pair, ahead = _chunk_pair_specs(l // CHUNK)
    per_batch = lambda r, w: pl.BlockSpec((1, r, w), lambda i, j: (i, 0, 0))
    return pl.pallas_call(
        _ssd_prompt_kernel,
        out_shape=(jax.ShapeDtypeStruct((b, l, D_MODEL), F32),
                   jax.ShapeDtypeStruct((b, SSD_INNER, SSD_STATE), F32),
                   jax.ShapeDtypeStruct((b, SUBLANE, SSD_CONV_DIM), F32)),
        grid=(b, l // (2 * CHUNK)),
        in_specs=[pair, ahead, const((1, D_MODEL)), const((D_MODEL, SSD_IN_W)),
                  const((SSD_CONV, SSD_CONV_DIM)), const((1, SSD_CONV_DIM)),
                  const((1, LANE)), const((1, LANE)), const((1, SSD_INNER)), const((1, SSD_INNER)),
                  const((SSD_INNER, D_MODEL))],
        out_specs=(pair, per_batch(SSD_INNER, SSD_STATE), per_batch(SUBLANE, SSD_CONV_DIM)),
        scratch_shapes=[pltpu.VMEM((SSD_GROUPS, SSD_STATE, SSD_GW), F32),
                        pltpu.VMEM((SUBLANE, SSD_CONV_DIM), F32),
                        pltpu.VMEM((CHUNK, SSD_IN_W), F32),
                        pltpu.VMEM((CHUNK, SSD_IN_W), F32)],
        compiler_params=pltpu.CompilerParams(dimension_semantics=("arbitrary", "arbitrary"),
                                             vmem_limit_bytes=VMEM_LIMIT),
        name="ssd_prompt",
    )(x, x, norm_mix, w_ssd_in, conv_w, conv_b, dt_bias, a_log, d_skip_x, ssd_norm, w_ssd_out)


def _merge_out(x, ssd_out, hg_pre, g_ssd, g_hg, whg_ref, wout_ref):
    hg_out = _dot(hg_pre, whg_ref[...])
    merged = _sigmoid(g_ssd) * ssd_out + _sigmoid(g_hg) * hg_out
    return x + _dot(merged, wout_ref[...])


def _hgrn_chunk(p_ref, x, ssd_out, lb_ref, hnorm_ref, whg_ref, wout_ref, st_ref):
    n = CHUNK
    seg = lambda i: p_ref[:, i * D_MODEL:(i + 1) * D_MODEL]
    lb = _lower_bound(lb_ref[...])
    q, k, logf = _hgrn_gates(seg(0), seg(1), lb)
    v = seg(2)
    g = _sel_dot_l(_cumsum_matrix(n, n), logf, parts=2)
    g_last = g[n - 1:n, :]
    scores = _hgrn_scores(q, k, g, n)
    q_in = (q * jnp.exp(g)).astype(BF16)
    k_out = (k * jnp.exp(g_last - g)).astype(BF16)
    s_dec = jnp.exp(g_last)
    vb = v.astype(BF16)
    outs = []
    for h in range(HG_HEADS):
        sl = slice(h * HG_K, (h + 1) * HG_K)
        st = st_ref[h]
        outs.append(_dot(scores[h], vb[:, sl]) + _dot_nt(q_in[:, sl], st))
        st_ref[h] = st * s_dec[:, sl] + _dot_tn(vb[:, sl], k_out[:, sl])
    o = _group_rms(jnp.concatenate(outs, axis=1), hnorm_ref[...], HG_HEADS) * _silu(seg(3))
    return _merge_out(x, ssd_out, o, seg(4), seg(5), whg_ref, wout_ref)


def _hgrn_prompt_kernel(xa_ref, xb_ref, ssd_ref, nw_ref, win_ref, lb_ref, hnorm_ref, whg_ref, wout_ref,
                        out_ref, hst_ref, st_ref, p0_ref, p1_ref):
    @pl.when(pl.program_id(1) == 0)
    def _():
        st_ref[...] = jnp.zeros_like(st_ref)

    def store(half, val):
        out_ref[0, half * CHUNK:(half + 1) * CHUNK, :] = val

    _skewed_chunks(
        xa_ref, xb_ref, p0_ref, p1_ref,
        lambda p_ref, x: _project_into(p_ref, x, nw_ref, win_ref, D_MODEL),
        lambda p_ref, x, half: _hgrn_chunk(p_ref, x, ssd_ref[0, half * CHUNK:(half + 1) * CHUNK, :], lb_ref,
                                           hnorm_ref, whg_ref, wout_ref, st_ref),
        store)

    @pl.when(pl.program_id(1) == pl.num_programs(1) - 1)
    def _():
        for h in range(HG_HEADS):
            hst_ref[0, h * HG_K:(h + 1) * HG_K, :] = st_ref[h].T


def _hgrn_prompt(x, ssd_out, norm_mix, w_hg_in, hgrn_lb, hgrn_norm, w_hgrn_out, w_out):
    b, l, _ = x.shape
    const = lambda shape: pl.BlockSpec(shape, lambda i, j: (0, 0), pipeline_mode=pl.Buffered(1))
    pair, ahead = _chunk_pair_specs(l // CHUNK)
    return pl.pallas_call(
        _hgrn_prompt_kernel,
        out_shape=(jax.ShapeDtypeStruct((b, l, D_MODEL), F32),
                   jax.ShapeDtypeStruct((b, HG_DIM, HG_V), F32)),
        grid=(b, l // (2 * CHUNK)),
        in_specs=[pair, ahead, pair, const((1, D_MODEL)), const((D_MODEL, HG_IN_W)), const((2, HG_DIM)),
                  const((1, HG_DIM)), const((HG_DIM, D_MODEL)), const((D_MODEL, D_MODEL))],
        out_specs=(pair, pl.BlockSpec((1, HG_DIM, HG_V), lambda i, j: (i, 0, 0))),
        scratch_shapes=[pltpu.VMEM((HG_HEADS, HG_V, HG_K), F32),
                        pltpu.VMEM((CHUNK, HG_IN_W), F32),
                        pltpu.VMEM((CHUNK, HG_IN_W), F32)],
        compiler_params=pltpu.CompilerParams(dimension_semantics=("arbitrary", "arbitrary"),
                                             vmem_limit_bytes=VMEM_LIMIT),
        name="hgrn_prompt",
    )(x, x, ssd_out, norm_mix, w_hg_in, hgrn_lb, hgrn_norm, w_hgrn_out, w_out)


def _attend_head(q, keys, values):
    s = _dot_nt(q, keys) * (XA_HEAD_DIM ** -0.5)
    p = jnp.exp(s - jnp.max(s, axis=-1, keepdims=True))
    p = p / jnp.sum(p, axis=-1, keepdims=True)
    return _dot(p, values)


def _attn_sample_kernel(q_ref, k_ref, v_ref, o_ref):
    nrow = MEM_LEN * XA_HEADS
    qrows = XA_HEADS * SLAB
    own = (_iota((qrows, nrow), 1) & (XA_HEADS - 1)) == _blk_id(_iota((qrows, nrow), 0), SLAB)
    for i in range(k_ref.shape[1]):
        rows = slice(i * SLAB, (i + 1) * SLAB)
        q = jnp.concatenate([q_ref[rows, h * XA_HEAD_DIM:(h + 1) * XA_HEAD_DIM] for h in range(XA_HEADS)], axis=0)
        s = _dot_nt(q, k_ref[0, i].reshape(nrow, XA_HEAD_DIM)) * (XA_HEAD_DIM ** -0.5)
        s = jnp.where(own, s, NEG)
        p = jnp.exp(s - jnp.max(s, axis=-1, keepdims=True))
        p = p / jnp.sum(p, axis=-1, keepdims=True)
        o = _dot(p, v_ref[0, i].reshape(nrow, XA_HEAD_DIM))
        for h in range(XA_HEADS):
            o_ref[rows, h * XA_HEAD_DIM:(h + 1) * XA_HEAD_DIM] = o[h * SLAB:(h + 1) * SLAB, :]


def _attn_sample(q, cache_k, cache_v, nb):
    b = cache_k.shape[1]
    kvspec = pl.BlockSpec((1, nb, MEM_LEN, XA_HEADS, XA_HEAD_DIM), lambda i: (0, i, 0, 0, 0))
    tok = pl.BlockSpec((nb * SLAB, D_MODEL), lambda i: (i, 0))
    return pl.pallas_call(
        _attn_sample_kernel,
        out_shape=jax.ShapeDtypeStruct((b * SLAB, D_MODEL), F32),
        grid=(b // nb,),
        in_specs=[tok, kvspec, kvspec],
        out_specs=tok,
        compiler_params=pltpu.CompilerParams(dimension_semantics=("arbitrary",),
                                             vmem_limit_bytes=VMEM_LIMIT),
        name="attn_sample",
    )(q, cache_k, cache_v)


def _co_ffn_tail(h, attn, wco_ref, nffn_ref, wg_ref, wu_ref, wd_ref, nfin_ref):
    h = h + _dot(attn, wco_ref[...])
    hn = _rms(h, nffn_ref[...]).astype(BF16)
    gate = jnp.dot(hn, wg_ref[...], preferred_element_type=F32)
    up = jnp.dot(hn, wu_ref[...], preferred_element_type=F32)
    h = h + _dot(_silu(gate) * up, wd_ref[...])
    return _rms(h, nfin_ref[...])


def _co_ffn_kernel(h_ref, a_ref, *refs):
    refs[-1][...] = _co_ffn_tail(h_ref[...], a_ref[...], *refs[:-1])


def _attn_ffn_kernel(h_ref, ncross_ref, wq_ref, kv_ref, *refs):
    h = h_ref[0]
    q = _dot(_rms(h, ncross_ref[...]), wq_ref[...])
    heads = []
    for i in range(XA_HEADS):
        sl = slice(i * XA_HEAD_DIM, (i + 1) * XA_HEAD_DIM)
        heads.append(_attend_head(q[:, sl], kv_ref[0, :, sl], kv_ref[0, :, D_MODEL + sl.start:D_MODEL + sl.stop]))
    refs[-1][0] = _co_ffn_tail(h, jnp.concatenate(heads, axis=1), *refs[:-1])


def _ffn_weight_specs():
    const = lambda shape: pl.BlockSpec(shape, lambda *_: (0, 0), pipeline_mode=pl.Buffered(1))
    return [const((D_MODEL, D_MODEL)), const((1, D_MODEL)), const((D_MODEL, FFN_DIM)), const((D_MODEL, FFN_DIM)),
            const((FFN_DIM, D_MODEL)), const((1, D_MODEL))]


def _co_ffn(h1, attn, ffn_weights, tm, name):
    m = h1.shape[0]
    tok = pl.BlockSpec((tm, D_MODEL), lambda i: (i, 0))
    return pl.pallas_call(
        _co_ffn_kernel,
        out_shape=jax.ShapeDtypeStruct((m, D_MODEL), F32),
        grid=(m // tm,),
        in_specs=[tok, tok] + _ffn_weight_specs(),
        out_specs=tok,
        compiler_params=pltpu.CompilerParams(dimension_semantics=("arbitrary",),
                                             vmem_limit_bytes=VMEM_LIMIT),
        name=name,
    )(h1, attn, *ffn_weights)


def _attn_ffn(h1, norm_cross, w_cq, kv, ffn_weights, tq, name):
    b, l, _ = h1.shape
    tok = pl.BlockSpec((1, tq, D_MODEL), lambda i, j: (i, j, 0))
    const = lambda shape: pl.BlockSpec(shape, lambda i, j: (0, 0), pipeline_mode=pl.Buffered(1))
    return pl.pallas_call(
        _attn_ffn_kernel,
        out_shape=jax.ShapeDtypeStruct((b, l, D_MODEL), F32),
        grid=(b, l // tq),
        in_specs=[tok, const((1, D_MODEL)), const((D_MODEL, D_MODEL)),
                  pl.BlockSpec((1, MEM_LEN, 2 * D_MODEL), lambda i, j: (i, 0, 0))] + _ffn_weight_specs(),
        out_specs=tok,
        compiler_params=pltpu.CompilerParams(dimension_semantics=("arbitrary", "arbitrary"),
                                             vmem_limit_bytes=VMEM_LIMIT),
        name=name,
    )(h1, norm_cross, w_cq, kv, *ffn_weights)


def _sample_pre_kernel(ps_ref, ph_ref, cst_ref, convw_ref, convb_ref, dtb_ref, alog_ref, dskip_ref, lb_ref,
                       ycore_ref, ecum_ref, dec_ref, cm_ref, bm_ref, xw_ref, conv_ref,
                       odiag_ref, qg_ref, kg_ref, v_ref, sdec_ref):
    n = ps_ref.shape[0]
    row = _iota((n, LANE), 0) & (SLAB - 1)
    valid = (row >= SLAB_TOK0) & (row < SLAB_TOK0 + SLAB_NTOK)

    full = cst_ref[...] + ps_ref[:, SSD_INNER:SSD_INNER + SSD_CONV_DIM]
    conv_ref[...] = full
    xa = _conv_silu(jnp.zeros((SUBLANE, SSD_CONV_DIM), F32), full, convw_ref[...], convb_ref[...])
    xs = xa[:, 0:SSD_INNER]
    bm = xa[:, SSD_INNER:SSD_INNER + SSD_GROUPS * SSD_STATE]
    cm = xa[:, SSD_INNER + SSD_GROUPS * SSD_STATE:SSD_CONV_DIM]

    csum = _cumsum_matrix(n, SLAB)
    dt, cum, ecum, wend, dec = _ssd_terms(ps_ref[:, SSD_INNER + SSD_CONV_DIM:SSD_IN_W], dtb_ref[...],
                                          alog_ref[...], csum, lambda a: _blk_row(a, SLAB, SLAB - 1), valid)
    wide = _expand_heads(jnp.concatenate([dt, wend, dec], axis=0), _head_expand_matrix())
    cgs = [cm[:, g * SSD_STATE:(g + 1) * SSD_STATE] for g in range(SSD_GROUPS)]
    bgs = [bm[:, g * SSD_STATE:(g + 1) * SSD_STATE] for g in range(SSD_GROUPS)]
    y = _ssd_intra(cum, lambda g: _dot_nt(cgs[g], bgs[g]), xs * wide[0:n], _causal_mask(n, SLAB))
    ycore_ref[...] = y + xs * dskip_ref[...]
    ecum_ref[...] = ecum
    dec_ref[...] = wide[2 * n:3 * n]
    cm_ref[...] = cm
    bm_ref[...] = bm
    xw_ref[...] = xs * wide[n:2 * n]

    lb = _lower_bound(lb_ref[...])
    q, k, logf = _hgrn_gates(ph_ref[:, 0:D_MODEL], ph_ref[:, D_MODEL:2 * D_MODEL], lb)
    v = ph_ref[:, 2 * D_MODEL:3 * D_MODEL]
    roww = _iota((n, HG_DIM), 0) & (SLAB - 1)
    validw = (roww >= SLAB_TOK0) & (roww < SLAB_TOK0 + SLAB_NTOK)
    k = jnp.where(validw, k, 0.0)
    logf = jnp.where(validw, logf, 0.0)
    g = _sel_dot_l(csum, logf)
    g_last = _blk_row(g, SLAB, SLAB - 1)
    scores = _hgrn_scores(q, k, g, SLAB)
    for h in range(HG_HEADS):
        sl = slice(h * HG_K, (h + 1) * HG_K)
        odiag_ref[:, sl] = _dot(scores[h], v[:, sl])
    qg_ref[...] = q * jnp.exp(g)
    kg_ref[...] = k * jnp.exp(g_last - g)
    v_ref[...] = v
    sdec_ref[...] = jnp.exp(g_last)


def _sample_pre(proj_ssd, proj_hg, conv_state, conv_w, conv_b, dt_bias, a_log, d_skip_x, hgrn_lb, tm):
    m = proj_ssd.shape[0]
    rows = lambda w: pl.BlockSpec((tm, w), lambda i: (i, 0))
    const = lambda shape: pl.BlockSpec(shape, lambda i: (0, 0))
    widths = (SSD_INNER, LANE, SSD_INNER, SSD_GROUPS * SSD_STATE, SSD_GROUPS * SSD_STATE, SSD_INNER,
              SSD_CONV_DIM, HG_DIM, HG_DIM, HG_DIM, HG_DIM, HG_DIM)
    return pl.pallas_call(
        _sample_pre_kernel,
        out_shape=tuple(jax.ShapeDtypeStruct((m, w), F32) for w in widths),
        grid=(m // tm,),
        in_specs=[rows(SSD_IN_W), rows(HG_IN_W), rows(SSD_CONV_DIM),
                  const((SSD_CONV, SSD_CONV_DIM)), const((1, SSD_CONV_DIM)), const((1, LANE)), const((1, LANE)),
                  const((1, SSD_INNER)), const((2, HG_DIM))],
        out_specs=tuple(rows(w) for w in widths),
        compiler_params=pltpu.CompilerParams(dimension_semantics=("arbitrary",),
                                             vmem_limit_bytes=VMEM_LIMIT),
        name="sample_pre",
    )(proj_ssd, proj_hg, conv_state, conv_w, conv_b, dt_bias, a_log, d_skip_x, hgrn_lb)


def _decay_rows(dec):
    hi, mid, lo = (p.astype(F32) for p in _split3(dec[0:1, :]))
    r = _iota((SLAB, dec.shape[1]), 0)
    return jnp.where(r == 0, hi, jnp.where(r == 1, mid, jnp.where(r == 2, lo, 0.0))).astype(BF16)


def _state_step(state, upd_rows, upd_cols, dec):
    qn = upd_cols.shape[1]
    lhs = jnp.concatenate([upd_rows.astype(BF16), _decay_rows(dec)], axis=0)
    ones3 = (_iota((SLAB, qn), 0) < 3).astype(BF16)
    zeros = jnp.zeros((SLAB, qn), BF16)
    rhs = jnp.concatenate([jnp.concatenate([upd_cols.astype(BF16), zeros], axis=1),
                           jnp.concatenate([zeros, ones3], axis=1)], axis=0)
    both = lax.dot_general(lhs, rhs, (((0,), (0,)), ((), ())), preferred_element_type=F32)
    return state * both[:, qn:2 * qn] + both[:, 0:qn]


def _sample_state_kernel(cm_ref, bm_ref, xw_ref, dec_ref, qg_ref, kg_ref, v_ref, sdec_ref, ssm_ref, hst_ref,
                         yint_ref, oint_ref, ssm_out_ref, hst_out_ref):
    for i in range(ssm_ref.shape[0]):
        rows = slice(i * SLAB, (i + 1) * SLAB)
        for g in range(SSD_GROUPS):
            hsl = slice(g * SSD_GW, (g + 1) * SSD_GW)
            nsl = slice(g * SSD_STATE, (g + 1) * SSD_STATE)
            st = ssm_ref[i, hsl, :]
            yint_ref[rows, hsl] = _dot_nt(cm_ref[rows, nsl], st)
            ssm_out_ref[i, hsl, :] = _state_step(st, xw_ref[rows, hsl], bm_ref[rows, nsl], dec_ref[rows, hsl])
        for h in range(HG_HEADS):
            sl = slice(h * HG_K, (h + 1) * HG_K)
            st = hst_ref[i, sl, :]
            oint_ref[rows, sl] = _dot(qg_ref[rows, sl], st)
            hst_out_ref[i, sl, :] = _state_step(st, kg_ref[rows, sl], v_ref[rows, sl], sdec_ref[rows, sl])


def _sample_state(cm, bm, xw, dec, qg, kg, v, sdec, ssm, hst, nb):
    b = ssm.shape[0]
    rows = lambda w: pl.BlockSpec((nb * SLAB, w), lambda i: (i, 0))
    ssm_spec = pl.BlockSpec((nb, SSD_INNER, SSD_STATE), lambda i: (i, 0, 0))
    hst_spec = pl.BlockSpec((nb, HG_DIM, HG_V), lambda i: (i, 0, 0))
    return pl.pallas_call(
        _sample_state_kernel,
        out_shape=(jax.ShapeDtypeStruct((b * SLAB, SSD_INNER), F32),
                   jax.ShapeDtypeStruct((b * SLAB, HG_DIM), F32),
                   jax.ShapeDtypeStruct(ssm.shape, F32),
                   jax.ShapeDtypeStruct(hst.shape, F32)),
        grid=(b // nb,),
        in_specs=[rows(SSD_GROUPS * SSD_STATE), rows(SSD_GROUPS * SSD_STATE), rows(SSD_INNER), rows(SSD_INNER),
                  rows(HG_DIM), rows(HG_DIM), rows(HG_DIM), rows(HG_DIM), ssm_spec, hst_spec],
        out_specs=(rows(SSD_INNER), rows(HG_DIM), ssm_spec, hst_spec),
        compiler_params=pltpu.CompilerParams(dimension_semantics=("arbitrary",),
                                             vmem_limit_bytes=VMEM_LIMIT),
        name="sample_state",
    )(cm, bm, xw, dec, qg, kg, v, sdec, ssm, hst)


def _sample_merge_kernel(x_ref, z_ref, gates_ref, ycore_ref, yint_ref, ecum_ref, odiag_ref, oint_ref,
                         snorm_ref, wso_ref, hnorm_ref, whg_ref, wout_ref, ncross_ref, wq_ref,
                         h_ref, q_ref):
    y = ycore_ref[...] + yint_ref[...] * _expand_heads(ecum_ref[...], _head_expand_matrix())
    y = y * _silu(z_ref[...])
    ssd_out = _dot(_group_rms(y, snorm_ref[...], SSD_GROUPS), wso_ref[...])
    o = _group_rms(odiag_ref[...] + oint_ref[...], hnorm_ref[...], HG_HEADS) * _silu(gates_ref[:, 0:D_MODEL])
    h = _merge_out(x_ref[...], ssd_out, o, gates_ref[:, D_MODEL:2 * D_MODEL], gates_ref[:, 2 * D_MODEL:3 * D_MODEL],
                   whg_ref, wout_ref)
    h_ref[...] = h
    q_ref[...] = _dot(_rms(h, ncross_ref[...]), wq_ref[...])


def _sample_merge(x, proj_ssd, proj_hg, ycore, yint, ecum, odiag, oint, ssd_norm, w_ssd_out, hgrn_norm,
                  w_hgrn_out, w_out, norm_cross, w_cq, tm):
    m = x.shape[0]
    rows = lambda w: pl.BlockSpec((tm, w), lambda i: (i, 0))
    const = lambda shape: pl.BlockSpec(shape, lambda i: (0, 0))
    return pl.pallas_call(
        _sample_merge_kernel,
        out_shape=(jax.ShapeDtypeStruct((m, D_MODEL), F32), jax.ShapeDtypeStruct((m, D_MODEL), F32)),
        grid=(m // tm,),
        in_specs=[rows(D_MODEL),
                  pl.BlockSpec((tm, SSD_INNER), lambda i: (i, 0)),
                  pl.BlockSpec((tm, 3 * D_MODEL), lambda i: (i, 1)),
                  rows(SSD_INNER), rows(SSD_INNER), rows(LANE), rows(HG_DIM), rows(HG_DIM),
                  const((1, SSD_INNER)), const((SSD_INNER, D_MODEL)), const((1, HG_DIM)),
                  const((HG_DIM, D_MODEL)), const((D_MODEL, D_MODEL)), const((1, D_MODEL)),
                  const((D_MODEL, D_MODEL))],
        out_specs=(rows(D_MODEL), rows(D_MODEL)),
        compiler_params=pltpu.CompilerParams(dimension_semantics=("arbitrary",),
                                             vmem_limit_bytes=VMEM_LIMIT),
        name="sample_merge",
    )(x, proj_ssd, proj_hg, ycore, yint, ecum, odiag, oint, ssd_norm, w_ssd_out, hgrn_norm, w_hgrn_out, w_out,
      norm_cross, w_cq)


def kernel(x_prompt, x_sample, mem_prompt, state_ssm, state_conv, state_hgrn, cache_mem_k, cache_mem_v,
           norm_mix, w_in, conv_w, conv_b, dt_bias, a_log, d_skip, ssd_norm, w_ssd_out, hgrn_lb, hgrn_norm,
           w_hgrn_out, w_out, norm_cross, norm_mem, w_cq, w_ck, w_cv, w_co, norm_ffn, w_gate, w_up, w_down,
           norm_final):
    assert norm_mix.shape[0] == 1, "single-layer configuration only"
    bp, lp, _ = x_prompt.shape
    bs, ls, _ = x_sample.shape
    assert ls == SLAB_NTOK and lp % CHUNK == 0

    row = lambda p: p[0].reshape(1, -1)
    head_lanes = lambda p: jnp.pad(jnp.tile(p, (1, HEAD_REPS)), ((0, 0), (0, LANE - HEAD_REPS * SSD_HEADS)))
    w_in0 = w_in[0]
    n_ssd = SSD_INNER + SSD_CONV_DIM
    w_ssd_in = jnp.concatenate([w_in0[:, :n_ssd], head_lanes(w_in0[:, n_ssd:n_ssd + SSD_HEADS])], axis=1).astype(BF16)
    w_hg_in = w_in0[:, n_ssd + SSD_HEADS:].astype(BF16)
    dtb = head_lanes(row(dt_bias))
    alog = head_lanes(row(a_log))
    d_skip_x = jnp.repeat(d_skip[0], SSD_HEAD_DIM).reshape(1, SSD_INNER)
    bf = lambda p: p[0].astype(BF16)
    w_so, w_ho, w_o, w_q, w_c = bf(w_ssd_out), bf(w_hgrn_out), bf(w_out), bf(w_cq), bf(w_co)
    w_g, w_u, w_d = bf(w_gate), bf(w_up), bf(w_down)
    w_kv = jnp.concatenate([w_ck[0], w_cv[0]], axis=1).astype(BF16)
    nfin = norm_final.reshape(1, D_MODEL)

    ffn_weights = (w_c, row(norm_ffn), w_g, w_u, w_d, nfin)
    kv, kv_bf = _rms_matmul(mem_prompt.reshape(bp * MEM_LEN, D_MODEL), row(norm_mem), w_kv, 256, "memory_kv",
                            (F32, BF16))
    ssd_out, ssm_p, conv_p = _ssd_prompt(x_prompt, row(norm_mix), w_ssd_in, conv_w[0], row(conv_b), dtb, alog,
                                         d_skip_x, row(ssd_norm), w_so)
    h1_p, hst_p = _hgrn_prompt(x_prompt, ssd_out, row(norm_mix), w_hg_in, hgrn_lb, row(hgrn_norm), w_ho, w_o)
    y_p = _attn_ffn(h1_p, row(norm_cross), w_q, kv_bf.reshape(bp, MEM_LEN, 2 * D_MODEL), ffn_weights, min(512, lp),
                    "attn_ffn_prompt")

    ms = bs * SLAB
    xs = jnp.pad(x_sample, ((0, 0), (SLAB_TOK0, SLAB - SLAB_TOK0 - ls), (0, 0))).reshape(ms, D_MODEL)
    cst = jnp.pad(state_conv[0], ((0, 0), (0, SLAB - (SSD_CONV - 1)), (0, 0))).reshape(ms, SSD_CONV_DIM)
    (proj_ssd,) = _rms_matmul(xs, row(norm_mix), w_ssd_in, 256, "sample_in_ssd")
    (proj_hg,) = _rms_matmul(xs, row(norm_mix), w_hg_in, 256, "sample_in_hgrn")
    (ycore, ecum, dec, cm, bm, xw, conv_full, odiag, qg, kg, vv, sdec) = _sample_pre(
        proj_ssd, proj_hg, cst, conv_w[0], row(conv_b), dtb, alog, d_skip_x, hgrn_lb, 128)
    yint, oint, ssm_s, hst_s = _sample_state(
        cm, bm, xw, dec, qg, kg, vv, sdec, state_ssm[0].reshape(bs, SSD_INNER, SSD_STATE),
        state_hgrn[0].reshape(bs, HG_DIM, HG_V), 4)
    h1_s, q_s = _sample_merge(xs, proj_ssd, proj_hg, ycore, yint, ecum, odiag, oint, row(ssd_norm), w_so,
                              row(hgrn_norm), w_ho, w_o, row(norm_cross), w_q, 256)
    att_s = _attn_sample(q_s, cache_mem_k, cache_mem_v, 4)
    y_s = _co_ffn(h1_s, att_s, ffn_weights, 256, "co_ffn_sample")

    tok = slice(SLAB_TOK0, SLAB_TOK0 + ls)
    kv4 = kv.reshape(bp, MEM_LEN, 2, XA_HEADS, XA_HEAD_DIM)
    return (y_p.reshape(bp, lp, D_MODEL),
            y_s.reshape(bs, SLAB, D_MODEL)[:, tok],
            ssm_p.reshape(1, bp, SSD_HEADS, SSD_HEAD_DIM, SSD_STATE),
            conv_p[None, :, SUBLANE - (SSD_CONV - 1):],
            hst_p.reshape(1, bp, HG_HEADS, HG_K, HG_V),
            kv4[:, :, 0][None],
            kv4[:, :, 1][None],
            ssm_s.reshape(1, bs, SSD_HEADS, SSD_HEAD_DIM, SSD_STATE),
            conv_full.reshape(bs, SLAB, SSD_CONV_DIM)[None, :, SLAB_TOK0 + ls - (SSD_CONV - 1):SLAB_TOK0 + ls],
            hst_s.reshape(1, bs, HG_HEADS, HG_K, HG_V))
```
